```python
import math
import jax
import jax.numpy as jnp
from jax import lax

D_MODEL = 1024
BATCH = 4
SEQ = 8192
DEPTH = 2

QBLOCK = 128
HEAD_DIM = 64
NSA_HEADS = 8
NSA_GROUPS = 2
NSA_HPG = NSA_HEADS // NSA_GROUPS
CMP_LEN = 32
CMP_STRIDE = 16
CMP_HIDDEN = 256
SEL_BLOCK = 64
N_SELECT = 16
WINDOW = 512
FORCE_BONUS = 1.0e4
DIFF_HEADS = 4
DIFF_DIM = 64
MLA_HEADS = 8
MLA_NOPE = 64
MLA_ROPE = 32
MLA_V = 64
Q_LORA = 256
KV_LORA = 128
ROPE_THETA = 10000.0
N_EXPERTS = 16
N_GROUPS = 4
EXPERTS_PER_GROUP = N_EXPERTS // N_GROUPS
TOP_K = 2
D_FF_EXPERT = 512
DEEPNORM_ALPHA = (2 * DEPTH) ** 0.25
DEEPNORM_BETA = (8 * DEPTH) ** -0.25
LN_EPS = 1e-5
RMS_EPS = 1e-6
NSA_WIDTH = NSA_HEADS * HEAD_DIM
NSA_KV = NSA_GROUPS * HEAD_DIM
DIFF_WIDTH = DIFF_HEADS * 2 * DIFF_DIM
MLA_WIDTH = MLA_HEADS * MLA_V
IN_SPLITS = (NSA_WIDTH,) + (NSA_KV,) * 6 + (3 * NSA_HEADS, DIFF_WIDTH, DIFF_WIDTH, DIFF_WIDTH, Q_LORA, KV_LORA, MLA_ROPE, 3 * D_MODEL)
IN_WIDTH = sum(IN_SPLITS)

kernel_name = 'hybrid_nsa_diff_mla_moe_deepnorm'


def layer_norm(x, g, b):
    xf = x.astype(jnp.float32)
    mu = jnp.mean(xf, -1, keepdims=True)
    var = jnp.mean(jnp.square(xf - mu), -1, keepdims=True)
    return ((xf - mu) * lax.rsqrt(var + LN_EPS)).astype(x.dtype) * g + b


def rms_norm(x, g):
    xf = x.astype(jnp.float32)
    return (xf * lax.rsqrt(jnp.mean(xf * xf, -1, keepdims=True) + RMS_EPS)).astype(x.dtype) * g


def masked_softmax(s, mask):
    s = jnp.where(mask, s, -1e30)
    mx = jnp.max(s, -1, keepdims=True)
    e = jnp.where(mask, jnp.exp(s - mx), 0.0)
    return e / jnp.maximum(jnp.sum(e, -1, keepdims=True), 1e-30)


def alibi_slopes():
    n = NSA_HEADS + DIFF_HEADS
    return 2.0 ** (-8.0 * jnp.arange(1, n + 1, dtype=jnp.float32) / n)


def rope(x, pos):
    half = x.shape[-1] // 2
    freqs = ROPE_THETA ** (-jnp.arange(half, dtype=jnp.float32) / half)
    ang = pos.astype(jnp.float32)[:, None] * freqs[None, :]
    shape = (1, x.shape[1]) + (1,) * (x.ndim - 3) + (half,)
    cos = jnp.cos(ang).reshape(shape).astype(x.dtype)
    sin = jnp.sin(ang).reshape(shape).astype(x.dtype)
    x1, x2 = x[..., :half], x[..., half:]
    return jnp.concatenate([x1 * cos - x2 * sin, x1 * sin + x2 * cos], -1)


def split_cols(a, widths):
    out, o = [], 0
    for w in widths:
        out.append(a[..., o:o + w])
        o += w
    return out


def sweep_query_blocks(block_fn, seq):
    n = seq // QBLOCK
    out = lax.map(block_fn, jnp.arange(n, dtype=jnp.int32) * QBLOCK)
    out = jnp.moveaxis(out, 0, 1)
    return out.reshape((out.shape[0], n * QBLOCK) + out.shape[3:])


def compress_blocks(x, pos_emb, w1, w2):
    B, S, G, Dh = x.shape
    ratio = CMP_LEN // CMP_STRIDE
    n_chunk = S // CMP_STRIDE
    n_cmp = n_chunk - ratio + 1
    chunks = x.reshape(B, n_chunk, CMP_STRIDE, G, Dh)
    blocks = jnp.concatenate([chunks[:, j:j + n_cmp] for j in range(ratio)], axis=2)
    blocks = blocks + pos_emb[:, None, :]
    flat = jnp.moveaxis(blocks, 3, 2).reshape(B, n_cmp, G, CMP_LEN * Dh)
    return jax.nn.gelu(flat @ w1) @ w2


def nsa_mixer(q, kc, vc, ks, vs, kw, vw, gates, cmp_pos_k, cmp_w1_k, cmp_w2_k, cmp_pos_v, cmp_w1_v, cmp_w2_v, slopes):
    B, S = q.shape[:2]
    G, Dh = NSA_GROUPS, HEAD_DIM
    scale = Dh ** -0.5
    ratio = CMP_LEN // CMP_STRIDE
    n_chunk = S // CMP_STRIDE
    n_sb = S // SEL_BLOCK
    n_sel = min(N_SELECT, n_sb)
    chunks_per_sb = SEL_BLOCK // CMP_STRIDE
    k_cmp = compress_blocks(kc, cmp_pos_k, cmp_w1_k, cmp_w2_k)
    v_cmp = compress_blocks(vc, cmp_pos_v, cmp_w1_v, cmp_w2_v)
    n_cmp = k_cmp.shape[1]
    cmp_end = jnp.arange(n_cmp) * CMP_STRIDE + (CMP_LEN - 1)
    ks_blk = ks.transpose(0, 2, 1, 3).reshape(B, G, n_sb, SEL_BLOCK, Dh)
    vs_blk = vs.transpose(0, 2, 1, 3).reshape(B, G, n_sb, SEL_BLOCK, Dh)
    kw_pad = jnp.pad(kw, ((0, 0), (WINDOW, 0), (0, 0), (0, 0)))
    vw_pad = jnp.pad(vw, ((0, 0), (WINDOW, 0), (0, 0), (0, 0)))
    slope = slopes.reshape(G, NSA_HPG)[None, :, :, None, None]
    sb_ids = jnp.arange(n_sb)
    gather_blocks = jax.vmap(jax.vmap(lambda a, i: a[i]))

    def block(q0):
        t = q0 + jnp.arange(QBLOCK)
        qb = lax.dynamic_slice_in_dim(q, q0, QBLOCK, 1).reshape(B, QBLOCK, G, NSA_HPG, Dh)
        gb = lax.dynamic_slice_in_dim(gates, q0, QBLOCK, 1).reshape(B, QBLOCK, G, NSA_HPG, 3)
        s_c = jnp.einsum('bqghd,bcgd->bghqc', qb, k_cmp).astype(jnp.float32) * scale
        dist_c = t[:, None] - cmp_end[None, :]
        p_c = masked_softmax(s_c - slope * dist_c.astype(jnp.float32), dist_c >= 0)
        o_c = jnp.einsum('bghqc,bcgd->bqghd', p_c.astype(q.dtype), v_cmp)
        p_grp = jnp.sum(p_c, 2)
        p_pad = jnp.pad(p_grp, ((0, 0), (0, 0), (0, 0), (ratio - 1, ratio - 1)))
        chunk = p_pad[..., ratio - 1:ratio - 1 + n_chunk]
        for j in range(1, ratio):
            chunk = chunk + p_pad[..., ratio - 1 - j:ratio - 1 - j + n_chunk]
        sb_score = chunk.reshape(B, G, QBLOCK, n_sb, chunks_per_sb).sum(-1)
        cur = t // SEL_BLOCK
        forced = (sb_ids[None, :] == 0) | (sb_ids[None, :] == cur[:, None]) | (sb_ids[None, :] == cur[:, None] - 1)
        started = sb_ids[None, :] * SEL_BLOCK <= t[:, None]
        sb_score = jnp.where(forced, sb_score + FORCE_BONUS, sb_score)
        sb_score = jnp.where(started, sb_score, -FORCE_BONUS)
        _, sel = lax.top_k(sb_score, n_sel)
        sel_flat = sel.reshape(B, G, QBLOCK * n_sel)
        k_sel = gather_blocks(ks_blk, sel_flat).reshape(B, G, QBLOCK, n_sel * SEL_BLOCK, Dh)
        v_sel = gather_blocks(vs_blk, sel_flat).reshape(B, G, QBLOCK, n_sel * SEL_BLOCK, Dh)
        pos_sel = (sel[..., None] * SEL_BLOCK + jnp.arange(SEL_BLOCK)).reshape(B, G, QBLOCK, n_sel * SEL_BLOCK)
        dist_s = (t[None, None, :, None] - pos_sel)[:, :, None]
        s_s = jnp.einsum('bqghd,bgqkd->bghqk', qb, k_sel).astype(jnp.float32) * scale
        p_s = masked_softmax(s_s - slope * dist_s.astype(jnp.float32), dist_s >= 0)
        o_s = jnp.einsum('bghqk,bgqkd->bqghd', p_s.astype(q.dtype), v_sel)
        k_w = lax.dynamic_slice_in_dim(kw_pad, q0, WINDOW + QBLOCK, 1)
        v_w = lax.dynamic_slice_in_dim(vw_pad, q0, WINDOW + QBLOCK, 1)
        s_pos = q0 - WINDOW + jnp.arange(WINDOW + QBLOCK)
        dist_w = t[:, None] - s_pos[None, :]
        mask_w = (dist_w >= 0) & (dist_w < WINDOW) & (s_pos[None, :] >= 0)
        s_w = jnp.einsum('bqghd,bkgd->bghqk', qb, k_w).astype(jnp.float32) * scale
        p_w = masked_softmax(s_w - slope * dist_w.astype(jnp.float32), mask_w)
        o_w = jnp.einsum('bghqk,bkgd->bqghd', p_w.astype(q.dtype), v_w)
        o = gb[..., 0:1] * o_c + gb[..., 1:2] * o_s + gb[..., 2:3] * o_w
        return o.reshape(B, QBLOCK, NSA_HEADS * Dh)

    return sweep_query_blocks(block, S)


def diff_mixer(q, k, v, lam_params, subln_g, lam_init, slopes):
    B, S = q.shape[:2]
    d = DIFF_DIM
    scale = d ** -0.5
    q1, q2 = q[..., :d], q[..., d:]
    k1, k2 = k[..., :d], k[..., d:]
    lp = lam_params.astype(jnp.float32)
    lam = jnp.exp(jnp.sum(lp[0] * lp[1])) - jnp.exp(jnp.sum(lp[2] * lp[3])) + lam_init
    slope = slopes[None, :, None, None]
    key_pos = jnp.arange(S)

    def block(q0):
        t = q0 + jnp.arange(QBLOCK)
        dist = t[:, None] - key_pos[None, :]
        mask = dist >= 0
        bias = -slope * dist.astype(jnp.float32)
        q1b = lax.dynamic_slice_in_dim(q1, q0, QBLOCK, 1)
        q2b = lax.dynamic_slice_in_dim(q2, q0, QBLOCK, 1)
        a1 = masked_softmax(jnp.einsum('bqhd,bkhd->bhqk', q1b, k1).astype(jnp.float32) * scale + bias, mask)
        a2 = masked_softmax(jnp.einsum('bqhd,bkhd->bhqk', q2b, k2).astype(jnp.float32) * scale + bias, mask)
        a = a1 - lam * a2
        return jnp.einsum('bhqk,bkhd->bqhd', a.astype(v.dtype), v)

    o = sweep_query_blocks(block, S)
    o = rms_norm(o, subln_g) * (1.0 - lam_init)
    return o.reshape(B, S, DIFF_HEADS * 2 * d)


def mla_mixer(c_q, c_kv, k_r, q_norm_g, kv_norm_g, w_uq, w_ukv):
    B, S = c_q.shape[:2]
    pos = jnp.arange(S)
    q = (rms_norm(c_q, q_norm_g) @ w_uq).reshape(B, S, MLA_HEADS, MLA_NOPE + MLA_ROPE)
    q_n, q_r = q[..., :MLA_NOPE], rope(q[..., MLA_NOPE:], pos)
    kv = (rms_norm(c_kv, kv_norm_g) @ w_ukv).reshape(B, S, MLA_HEADS, MLA_NOPE + MLA_V)
    k_n, v = kv[..., :MLA_NOPE], kv[..., MLA_NOPE:]
    k_r = rope(k_r, pos)
    scale = (MLA_NOPE + MLA_ROPE) ** -0.5
    key_pos = jnp.arange(S)

    def block(q0):
        t = q0 + jnp.arange(QBLOCK)
        qn = lax.dynamic_slice_in_dim(q_n, q0, QBLOCK, 1)
        qr = lax.dynamic_slice_in_dim(q_r, q0, QBLOCK, 1)
        s = (jnp.einsum('bqhd,bkhd->bhqk', qn, k_n) + jnp.einsum('bqhr,bkr->bhqk', qr, k_r)).astype(jnp.float32) * scale
        p = masked_softmax(s, key_pos[None, :] <= t[:, None])
        return jnp.einsum('bhqk,bkhd->bqhd', p.astype(v.dtype), v)

    o = sweep_query_blocks(block, S)
    return o.reshape(B, S, MLA_HEADS * MLA_V)


def mixer_block(h, layer, w_in, cmp_pos_k, cmp_w1_k, cmp_w2_k, cmp_pos_v, cmp_w1_v, cmp_w2_v, diff_lambda, diff_subln_g, mla_q_norm_g, mla_kv_norm_g, mla_w_uq, mla_w_ukv, w_br_nsa, w_br_diff, w_br_mla, w_out):
    B, S, _ = h.shape
    nq, kc, vc, ks, vs, kw, vw, ng, dq, dk, dv, cq, ckv, kr, mg = split_cols(h @ w_in, IN_SPLITS)

    def heads(a, n):
        return a.reshape(B, S, n, -1)

    slopes = alibi_slopes()
    o_nsa = nsa_mixer(heads(nq, NSA_HEADS), heads(kc, NSA_GROUPS), heads(vc, NSA_GROUPS), heads(ks, NSA_GROUPS), heads(vs, NSA_GROUPS), heads(kw, NSA_GROUPS), heads(vw, NSA_GROUPS), jax.nn.sigmoid(heads(ng, NSA_HEADS)), cmp_pos_k, cmp_w1_k, cmp_w2_k, cmp_pos_v, cmp_w1_v, cmp_w2_v, slopes[:NSA_HEADS])
    lam_init = 0.8 - 0.6 * math.exp(-0.3 * layer)
    o_diff = diff_mixer(heads(dq, DIFF_HEADS), heads(dk, DIFF_HEADS), heads(dv, DIFF_HEADS), diff_lambda, diff_subln_g, lam_init, slopes[NSA_HEADS:])
    o_mla = mla_mixer(cq, ckv, kr, mla_q_norm_g, mla_kv_norm_g, mla_w_uq, mla_w_ukv)
    g = jax.nn.sigmoid(mg).reshape(B, S, 3, D_MODEL)
    y = g[:, :, 0] * (o_nsa @ w_br_nsa) + g[:, :, 1] * (o_diff @ w_br_diff) + g[:, :, 2] * (o_mla @ w_br_mla)
    return y @ w_out


def moe_ffn(h, router_w, router_b, w1, w3, w2):
    B, S, D = h.shape
    hf = h.reshape(B * S, D)
    aff = jax.nn.sigmoid((hf @ router_w).astype(jnp.float32))
    sel = (aff + router_b.astype(jnp.float32)).reshape(-1, N_GROUPS, EXPERTS_PER_GROUP)
    grp_score = jnp.sum(lax.top_k(sel, TOP_K)[0], -1)
    grp = jnp.argmax(grp_score, -1)
    in_grp = jnp.einsum('tge,tg->te', sel, jax.nn.one_hot(grp, N_GROUPS, dtype=jnp.float32))
    _, local = lax.top_k(in_grp, TOP_K)
    eidx = grp[:, None] * EXPERTS_PER_GROUP + local
    gate = jnp.take_along_axis(aff, eidx, axis=1)
    gate = gate / jnp.sum(gate, -1, keepdims=True)
    combine = jnp.einsum('tk,tke->te', gate, jax.nn.one_hot(eidx, N_EXPERTS, dtype=jnp.float32)).astype(h.dtype)
    y = jnp.zeros_like(hf)
    for e in range(N_EXPERTS):
        y = y + combine[:, e:e + 1] * ((jax.nn.silu(hf @ w1[e]) * (hf @ w3[e])) @ w2[e])
    return y.reshape(B, S, D)


def setup_inputs(seed: int = 0) -> dict:
    key = jax.random.key(seed)
    ks = jax.random.split(key, 32)
    L, D = DEPTH, D_MODEL

    def nrm(k, shape, scale):
        return jax.random.normal(k, shape, jnp.float32) * scale

    return {
        'x': nrm(ks[0], (BATCH, SEQ, D), 1.0),
        'ln_in_g': 1.0 + nrm(ks[1], (D,), 0.02),
        'ln_in_b': nrm(ks[2], (D,), 0.02),
        'w_in': nrm(ks[3], (L, D, IN_WIDTH), D ** -0.5),
        'cmp_pos_k': nrm(ks[4], (L, CMP_LEN, HEAD_DIM), 0.1),
        'cmp_w1_k': nrm(ks[5], (L, CMP_LEN * HEAD_DIM, CMP_HIDDEN), (CMP_LEN * HEAD_DIM) ** -0.5),
        'cmp_w2_k': nrm(ks[6], (L, CMP_HIDDEN, HEAD_DIM), CMP_HIDDEN ** -0.5),
        'cmp_pos_v': nrm(ks[7], (L, CMP_LEN, HEAD_DIM), 0.1),
        'cmp_w1_v': nrm(ks[8], (L, CMP_LEN * HEAD_DIM, CMP_HIDDEN), (CMP_LEN * HEAD_DIM) ** -0.5),
        'cmp_w2_v': nrm(ks[9], (L, CMP_HIDDEN, HEAD_DIM), CMP_HIDDEN ** -0.5),
        'diff_lambda': nrm(ks[10], (L, 4, DIFF_DIM), 0.1),
        'diff_subln_g': 1.0 + nrm(ks[11], (L, 2 * DIFF_DIM), 0.02),
        'mla_q_norm_g': 1.0 + nrm(ks[12], (L, Q_LORA), 0.02),
        'mla_kv_norm_g': 1.0 + nrm(ks[13], (L, KV_LORA), 0.02),
        'mla_w_uq': nrm(ks[14], (L, Q_LORA, MLA_HEADS * (MLA_NOPE + MLA_ROPE)), Q_LORA ** -0.5),
        'mla_w_ukv': nrm(ks[15], (L, KV_LORA, MLA_HEADS * (MLA_NOPE + MLA_V)), KV_LORA ** -0.5),
        'w_br_nsa': nrm(ks[16], (L, NSA_WIDTH, D), NSA_WIDTH ** -0.5),
        'w_br_diff': nrm(ks[17], (L, DIFF_WIDTH, D), DIFF_WIDTH ** -0.5),
        'w_br_mla': nrm(ks[18], (L, MLA_WIDTH, D), MLA_WIDTH ** -0.5),
        'w_out': nrm(ks[19], (L, D, D), D ** -0.5 * DEEPNORM_BETA),
        'ln1_g': 1.0 + nrm(ks[20], (L, D), 0.02),
        'ln1_b': nrm(ks[21], (L, D), 0.02),
        'router_w': nrm(ks[22], (D, N_EXPERTS), D ** -0.5),
        'router_b': nrm(ks[23], (N_EXPERTS,), 0.01),
        'moe_w1': nrm(ks[24], (L, N_EXPERTS, D, D_FF_EXPERT), D ** -0.5),
        'moe_w3': nrm(ks[25], (L, N_EXPERTS, D, D_FF_EXPERT), D ** -0.5),
        'moe_w2': nrm(ks[26], (L, N_EXPERTS, D_FF_EXPERT, D), D_FF_EXPERT ** -0.5 * DEEPNORM_BETA),
        'ln2_g': 1.0 + nrm(ks[27], (L, D), 0.02),
        'ln2_b': nrm(ks[28], (L, D), 0.02),
    }


def reference(x, ln_in_g, ln_in_b, w_in, cmp_pos_k, cmp_w1_k, cmp_w2_k, cmp_pos_v, cmp_w1_v, cmp_w2_v, diff_lambda, diff_subln_g, mla_q_norm_g, mla_kv_norm_g, mla_w_uq, mla_w_ukv, w_br_nsa, w_br_diff, w_br_mla, w_out, ln1_g, ln1_b, router_w, router_b, moe_w1, moe_w3, moe_w2, ln2_g, ln2_b):
    h = layer_norm(x, ln_in_g, ln_in_b)
    for l in range(DEPTH):
        mix = mixer_block(h, l, w_in[l], cmp_pos_k[l], cmp_w1_k[l], cmp_w2_k[l], cmp_pos_v[l], cmp_w1_v[l], cmp_w2_v[l], diff_lambda[l], diff_subln_g[l], mla_q_norm_g[l], mla_kv_norm_g[l], mla_w_uq[l], mla_w_ukv[l], w_br_nsa[l], w_br_diff[l], w_br_mla[l], w_out[l])
        h = layer_norm(DEEPNORM_ALPHA * h + mix, ln1_g[l], ln1_b[l])
        ffn = moe_ffn(h, router_w, router_b, moe_w1[l], moe_w3[l], moe_w2[l])
        h = layer_norm(DEEPNORM_ALPHA * h + ffn, ln2_g[l], ln2_b[l])
    return h
```

```python
import functools
import math

import jax
import jax.numpy as jnp
from jax import lax
from jax.experimental import pallas as pl
from jax.experimental.pallas import tpu as pltpu

F32 = jnp.float32
BF16 = jnp.bfloat16

LANES = 128
HEAD_DIM = 64
NSA_HEADS = 8
NSA_GROUPS = 2
NSA_HPG = NSA_HEADS // NSA_GROUPS
CMP_LEN = 32
CMP_STRIDE = 16
CMP_HIDDEN = 256
SEL_BLOCK = 64
N_SELECT = 16
WINDOW = 512
FORCE_BONUS = 1.0e4
DIFF_HEADS = 4
DIFF_DIM = 64
MLA_HEADS = 8
MLA_NOPE = 64
MLA_ROPE = 32
MLA_V = 64
Q_LORA = 256
KV_LORA = 128
ROPE_THETA = 10000.0
N_EXPERTS = 16
N_GROUPS = 4
EXPERTS_PER_GROUP = N_EXPERTS // N_GROUPS
D_FF_EXPERT = 512
LN_EPS = 1e-5
RMS_EPS = 1e-6
NEG = -1e30

VMEM_LIMIT = 56 * 1024 * 1024

ZB_QN, ZB_KS, ZB_VS, ZB_KW, ZB_VW, ZB_DQ, ZB_DK, ZB_DV, ZB_WIDTH = 0, 1024, 1280, 1536, 1792, 2048, 3072, 3584, 4096
ZF_CQ, ZF_KC, ZF_VC, ZF_NG, ZF_CKV, ZF_KR, ZF_KRR, ZF_MG, ZF_WIDTH = 0, 256, 384, 512, 640, 768, 896, 1024, 4096


def _alibi_slopes():
    n = NSA_HEADS + DIFF_HEADS
    return [2.0 ** (-8.0 * i / n) for i in range(1, n + 1)]


def _params(sem):
    return pltpu.CompilerParams(dimension_semantics=sem, vmem_limit_bytes=VMEM_LIMIT)


def _layer_norm(x, g, b):
    mu = jnp.mean(x, -1, keepdims=True)
    xc = x - mu
    var = jnp.mean(xc * xc, -1, keepdims=True)
    return xc * lax.rsqrt(var + LN_EPS) * g + b


def _rms_norm(x, g):
    return x * lax.rsqrt(jnp.mean(x * x, -1, keepdims=True) + RMS_EPS) * g


def _dot(a, b):
    return jnp.dot(a, b, preferred_element_type=F32)


def _dot_nt(a, b):
    return lax.dot_general(a, b, (((1,), (1,)), ((), ())), preferred_element_type=F32)


def _ln_kernel(x_ref, g_ref, b_ref, of_ref, ob_ref):
    y = _layer_norm(x_ref[...], g_ref[...], b_ref[...])
    of_ref[...] = y
    ob_ref[...] = y.astype(BF16)


def _ln_call(x, g, b, tm):
    T, D = x.shape
    row = pl.BlockSpec((tm, D), lambda i: (i, 0))
    vec = pl.BlockSpec((1, D), lambda i: (0, 0))
    return pl.pallas_call(
        _ln_kernel, grid=(T // tm,), in_specs=[row, vec, vec], out_specs=[row, row],
        out_shape=[jax.ShapeDtypeStruct((T, D), F32), jax.ShapeDtypeStruct((T, D), BF16)],
        compiler_params=_params(("parallel",)), name="ln_in")(x, g.reshape(1, D), b.reshape(1, D))


def _mm_kernel(a_ref, w_ref, o_ref):
    o_ref[...] = _dot(a_ref[...], w_ref[...]).astype(o_ref.dtype)


def _matmul(a, w, out_dtype, tm, tn, name):
    M, K = a.shape
    N = w.shape[1]
    return pl.pallas_call(
        _mm_kernel, grid=(M // tm, N // tn),
        in_specs=[pl.BlockSpec((tm, K), lambda i, j: (i, 0)), pl.BlockSpec((K, tn), lambda i, j: (0, j))],
        out_specs=pl.BlockSpec((tm, tn), lambda i, j: (i, j)),
        out_shape=jax.ShapeDtypeStruct((M, N), out_dtype),
        compiler_params=_params(("parallel", "arbitrary")), name=name)(a, w)


def _compress_kernel(x_ref, pos_ref, w1_ref, w2_ref, o_ref):
    flat = (x_ref[...] + pos_ref[...]).astype(BF16)
    hid = jax.nn.gelu(_dot(flat, w1_ref[...]))
    o_ref[...] = _dot(hid.astype(BF16), w2_ref[...]).astype(o_ref.dtype)


def _compress_call(flat, pos_flat, w1, w2p, tm, name):
    R, W = flat.shape
    return pl.pallas_call(
        _compress_kernel, grid=(R // tm,),
        in_specs=[pl.BlockSpec((tm, W), lambda i: (i, 0)), pl.BlockSpec((1, W), lambda i: (0, 0)),
                  pl.BlockSpec(w1.shape, lambda i: (0, 0)), pl.BlockSpec(w2p.shape, lambda i: (0, 0))],
        out_specs=pl.BlockSpec((tm, LANES), lambda i: (i, 0)),
        out_shape=jax.ShapeDtypeStruct((R, LANES), BF16),
        compiler_params=_params(("parallel",)), name=name)(flat, pos_flat, w1, w2p)


def _mla_prep_kernel(cq_ref, ckv_ref, kr_ref, krr_ref, cq128_ref, sq128_ref, ck128_ref, sk128_ref,
                     qg_ref, kvg_ref, wq_ref, wqr_ref, wk_ref, wv_ref, q_out, k_out, v_out, *, scale):
    cqn = _rms_norm(cq_ref[...], qg_ref[...]).astype(BF16)
    q_main = _dot(cqn, wq_ref[...])
    q_rot = _dot(cqn, wqr_ref[...])
    ckvn = _rms_norm(ckv_ref[...], kvg_ref[...]).astype(BF16)
    k_nope = _dot(ckvn, wk_ref[...])
    v_out[...] = _dot(ckvn, wv_ref[...]).astype(BF16)
    k_rope = kr_ref[...] * ck128_ref[...] + krr_ref[...] * sk128_ref[...]
    cq128 = cq128_ref[...]
    sq128 = sq128_ref[...]
    for h in range(MLA_HEADS):
        sl = slice(h * LANES, (h + 1) * LANES)
        q_out[:, sl] = ((q_main[:, sl] * cq128 + q_rot[:, sl] * sq128) * scale).astype(BF16)
        k_out[:, sl] = (k_nope[:, sl] + k_rope).astype(BF16)


def _mla_prep_call(zf, tabs, qg, kvg, wq, wqr, wk, wv, S, tm):
    T = zf.shape[0]
    npos = S // tm
    HW = MLA_HEADS * LANES

    def col(width, off):
        return pl.BlockSpec((tm, width), lambda i: (i, off // width))

    tab = pl.BlockSpec((tm, LANES), lambda i: (i % npos, 0))

    def full(a):
        return pl.BlockSpec(a.shape, lambda i: (0, 0))

    out = pl.BlockSpec((tm, HW), lambda i: (i, 0))
    return pl.pallas_call(
        functools.partial(_mla_prep_kernel, scale=(MLA_NOPE + MLA_ROPE) ** -0.5), grid=(T // tm,),
        in_specs=[col(Q_LORA, ZF_CQ), col(KV_LORA, ZF_CKV), col(LANES, ZF_KR), col(LANES, ZF_KRR),
                  tab, tab, tab, tab, full(qg), full(kvg), full(wq), full(wqr), full(wk), full(wv)],
        out_specs=[out, out, out],
        out_shape=[jax.ShapeDtypeStruct((T, HW), BF16)] * 3,
        compiler_params=_params(("parallel",)), name="mla_prep")(
            zf, zf, zf, zf, *tabs, qg, kvg, wq, wqr, wk, wv)


def _cmp_kernel(q_ref, kc_ref, vc_ref, mt_ref, o_ref, sel_ref, *, slopes, tq, nch, nsb, nsel):
    i = pl.program_id(1)
    t0 = i * tq
    r = lax.broadcasted_iota(jnp.int32, (tq, nch), 0)
    c = lax.broadcasted_iota(jnp.int32, (tq, nch), 1)
    dist = t0 + r - (CMP_STRIDE * c + (CMP_LEN - 1))
    valid = (dist >= 0) & (c < nch - 1)
    dist_f = dist.astype(F32)
    sb = lax.broadcasted_iota(jnp.int32, (LANES, tq), 0)
    sb_f = sb.astype(F32)
    t = t0 + lax.broadcasted_iota(jnp.int32, (LANES, tq), 1)
    cur = lax.shift_right_arithmetic(t, SEL_BLOCK.bit_length() - 1)
    forced = (sb == 0) | (sb == cur) | (sb == cur - 1)
    started = sb * SEL_BLOCK <= t
    mt = mt_ref[...]
    for g in range(NSA_GROUPS):
        kc = kc_ref[0, g]
        vc = vc_ref[0, g]
        p_grp = jnp.zeros((tq, nch), F32)
        for hh in range(NSA_HPG):
            h = g * NSA_HPG + hh
            sl = slice(h * LANES, (h + 1) * LANES)
            s = _dot_nt(q_ref[0, :, sl], kc)
            z = jnp.where(valid, s - slopes[h] * dist_f, NEG)
            mx = jnp.max(z, -1, keepdims=True)
            e = jnp.where(valid, jnp.exp(z - mx), 0.0)
            p = e / jnp.maximum(jnp.sum(e, -1, keepdims=True), 1e-30)
            o_ref[0, :, sl] = _dot(p.astype(BF16), vc).astype(BF16)
            p_grp = p_grp + p
        p1 = p_grp.astype(BF16)
        r1 = p_grp - p1.astype(F32)
        p2 = r1.astype(BF16)
        p3 = (r1 - p2.astype(F32)).astype(BF16)
        sc = _dot_nt(mt, p1) + _dot_nt(mt, p2) + _dot_nt(mt, p3)
        sc = jnp.where(forced, sc + FORCE_BONUS, sc)
        sc = jnp.where(started, sc, -FORCE_BONUS)
        sc = jnp.where(sb < nsb, sc, -jnp.inf)
        sel_t = jnp.zeros((LANES, tq), F32)
        for _ in range(nsel):
            m = jnp.max(sc, axis=0, keepdims=True)
            cand = jnp.where(sc == m, sb_f, float(LANES))
            idx = jnp.min(cand, axis=0, keepdims=True)
            hit = sb_f == idx
            sel_t = jnp.where(hit, 1.0, sel_t)
            sc = jnp.where(hit, -jnp.inf, sc)
        sel_ref[0, g] = sel_t.T.astype(BF16)


def _cmp_call(zb3, kcmp, vcmp, mt, slopes, tq):
    B, S, _ = zb3.shape
    nch = S // CMP_STRIDE
    nsb = S // SEL_BLOCK
    HW = NSA_HEADS * LANES
    kern = functools.partial(_cmp_kernel, slopes=slopes, tq=tq, nch=nch, nsb=nsb, nsel=min(N_SELECT, nsb))
    cmp_spec = pl.BlockSpec((1, NSA_GROUPS, nch, LANES), lambda b, i: (b, 0, 0, 0))
    return pl.pallas_call(
        kern, grid=(B, S // tq),
        in_specs=[pl.BlockSpec((1, tq, HW), lambda b, i: (b, i, ZB_QN // HW)), cmp_spec, cmp_spec,
                  pl.BlockSpec(mt.shape, lambda b, i: (0, 0))],
        out_specs=[pl.BlockSpec((1, tq, HW), lambda b, i: (b, i, 0)),
                   pl.BlockSpec((1, NSA_GROUPS, tq, LANES), lambda b, i: (b, 0, i, 0))],
        out_shape=[jax.ShapeDtypeStruct((B, S, HW), BF16),
                   jax.ShapeDtypeStruct((B, NSA_GROUPS, S, LANES), BF16)],
        compiler_params=_params(("parallel", "parallel")), name="nsa_cmp")(zb3, kcmp, vcmp, mt)


def _flash_kernel(*refs, n_heads, kmap, vmap, slopes, mode, has_sel, finalize, tq, tk, lam_init):
    it = iter(refs)
    q_ref, k_ref, v_ref, rel_ref = next(it), next(it), next(it), next(it)
    if has_sel:
        sel_ref, e_ref = next(it), next(it)
    if finalize == "diff":
        lam_ref, subg_ref = next(it), next(it)
    o_ref, m_ref, l_ref, acc_ref = next(it), next(it), next(it), next(it)

    i = pl.program_id(1)
    j = pl.program_id(2)
    if mode == "causal":
        kv = j
        is_diag = j == i
        is_off = j < i
    else:
        kv = i - 1 + j
        is_diag = j == 1
        is_off = (j == 0) & (kv >= 0)

    @pl.when(j == 0)
    def _init():
        m_ref[...] = jnp.full(m_ref.shape, NEG, F32)
        l_ref[...] = jnp.zeros(l_ref.shape, F32)
        acc_ref[...] = jnp.zeros(acc_ref.shape, F32)

    def step(diag):
        rel = rel_ref[...]
        if diag:
            base_mask = rel <= 0
        elif mode == "window":
            base_mask = rel > 0
        else:
            base_mask = None
        tile_off = ((kv - i) * tq).astype(F32)
        grp_mask = {}
        for h in range(n_heads):
            sl = slice(h * LANES, (h + 1) * LANES)
            q = q_ref[0, :, sl]
            k = k_ref[0, :, kmap[h] * LANES:(kmap[h] + 1) * LANES]
            v = v_ref[0, :, vmap[h] * LANES:(vmap[h] + 1) * LANES]
            u = _dot_nt(q, k)
            if slopes[h] != 0.0:
                u = u + slopes[h] * rel
                delta = slopes[h] * tile_off
            else:
                delta = 0.0
            mask = base_mask
            if has_sel:
                g = h // NSA_HPG
                if g not in grp_mask:
                    sm = _dot(sel_ref[0, g], e_ref[...]) > 0.5
                    grp_mask[g] = sm if base_mask is None else (sm & base_mask)
                mask = grp_mask[g]
            if mask is not None:
                u = jnp.where(mask, u, NEG)
            m_prev = m_ref[h]
            m_new = jnp.maximum(m_prev, jnp.max(u, -1, keepdims=True) + delta)
            alpha = jnp.exp(m_prev - m_new)
            shift = m_new - delta
            psum = None
            chunks = []
            for c in range(tk // LANES):
                pc = jnp.exp(u[:, c * LANES:(c + 1) * LANES] - shift)
                psum = pc if psum is None else psum + pc
                chunks.append(pc.astype(BF16))
            p = jnp.concatenate(chunks, axis=1)
            l_ref[h] = alpha * l_ref[h] + psum
            acc_ref[h] = alpha * acc_ref[h] + _dot(p, v)
            m_ref[h] = m_new

    def normalised(h):
        l = jnp.maximum(jnp.sum(l_ref[h], -1, keepdims=True), 1e-30)
        return acc_ref[h] / l

    @pl.when(is_off)
    def _off():
        step(False)

    @pl.when(is_diag)
    def _diag():
        step(True)
        if finalize == "plain":
            for h in range(n_heads):
                o_ref[0, :, h * LANES:(h + 1) * LANES] = normalised(h).astype(o_ref.dtype)
        else:
            lp = lam_ref[...]
            lam = (jnp.exp(jnp.sum(lp[0:1] * lp[1:2], -1, keepdims=True))
                   - jnp.exp(jnp.sum(lp[2:3] * lp[3:4], -1, keepdims=True)) + lam_init)
            for hd in range(n_heads // 2):
                o = normalised(2 * hd) - lam * normalised(2 * hd + 1)
                o = _rms_norm(o, subg_ref[...]) * (1.0 - lam_init)
                o_ref[0, :, hd * LANES:(hd + 1) * LANES] = o.astype(o_ref.dtype)


def _flash_call(q_arr, q_off, n_heads, k_arr, k_off, k_blocks, v_arr, v_off, v_blocks, rel, *, kmap, vmap, slopes,
                mode, tq, name, sel=None, emat=None, lam=None, subg=None, lam_init=0.0):
    B, S, _ = q_arr.shape
    tk = tq
    nq = S // tq
    QW, KW, VW = n_heads * LANES, k_blocks * LANES, v_blocks * LANES
    if mode == "causal":
        nsteps = nq
        kv_idx = lambda i, j: jnp.minimum(j, i)
    else:
        assert tq == WINDOW
        nsteps = 2
        kv_idx = lambda i, j: jnp.maximum(i - 1 + j, 0)
    finalize = "diff" if lam is not None else "plain"
    out_heads = n_heads // 2 if finalize == "diff" else n_heads
    in_specs = [pl.BlockSpec((1, tq, QW), lambda b, i, j: (b, i, q_off // QW)),
                pl.BlockSpec((1, tk, KW), lambda b, i, j: (b, kv_idx(i, j), k_off // KW)),
                pl.BlockSpec((1, tk, VW), lambda b, i, j: (b, kv_idx(i, j), v_off // VW)),
                pl.BlockSpec((tq, tk), lambda b, i, j: (0, 0))]
    args = [q_arr, k_arr, v_arr, rel]
    if sel is not None:
        in_specs += [pl.BlockSpec((1, NSA_GROUPS, tq, LANES), lambda b, i, j: (b, 0, i, 0)),
                     pl.BlockSpec((LANES, tk), lambda b, i, j: (0, kv_idx(i, j)))]
        args += [sel, emat]
    if finalize == "diff":
        in_specs += [pl.BlockSpec(lam.shape, lambda b, i, j: (0, 0)), pl.BlockSpec(subg.shape, lambda b, i, j: (0, 0))]
        args += [lam, subg]
    kern = functools.partial(_flash_kernel, n_heads=n_heads, kmap=kmap, vmap=vmap, slopes=slopes, mode=mode,
                             has_sel=sel is not None, finalize=finalize, tq=tq, tk=tk, lam_init=lam_init)
    stat = pltpu.VMEM((n_heads, tq, LANES), F32)
    return pl.pallas_call(
        kern, grid=(B, nq, nsteps), in_specs=in_specs,
        out_specs=pl.BlockSpec((1, tq, out_heads * LANES), lambda b, i, j: (b, i, 0)),
        out_shape=jax.ShapeDtypeStruct((B, S, out_heads * LANES), BF16),
        scratch_shapes=[stat, stat, stat],
        compiler_params=_params(("parallel", "parallel", "arbitrary")), name=name)(*args)


def _combine_kernel(oc_ref, os_ref, ow_ref, ng_ref, od_ref, om_ref, mg0_ref, mg1_ref, mg2_ref, h_ref,
                    eg_ref, wn_ref, wd_ref, wm_ref, wo_ref, g_ref, b_ref, of_ref, ob_ref, *, alpha):
    sg = jax.nn.sigmoid(ng_ref[...]).astype(BF16)
    o_nsa = (_dot(sg, eg_ref[0]) * oc_ref[...].astype(F32)
             + _dot(sg, eg_ref[1]) * os_ref[...].astype(F32)
             + _dot(sg, eg_ref[2]) * ow_ref[...].astype(F32))
    y = (jax.nn.sigmoid(mg0_ref[...]) * _dot(o_nsa.astype(BF16), wn_ref[...])
         + jax.nn.sigmoid(mg1_ref[...]) * _dot(od_ref[...], wd_ref[...])
         + jax.nn.sigmoid(mg2_ref[...]) * _dot(om_ref[...], wm_ref[...]))
    mix = _dot(y.astype(BF16), wo_ref[...])
    hn = _layer_norm(alpha * h_ref[...] + mix, g_ref[...], b_ref[...])
    of_ref[...] = hn
    ob_ref[...] = hn.astype(BF16)


def _combine_call(oc, os_, ow, zf, od, om, h, eg, wn, wd, wm, wo, g, b, alpha, tm):
    T, D = h.shape

    def row(width, blk=0):
        return pl.BlockSpec((tm, width), lambda i: (i, blk))

    def full(a):
        nd = a.ndim
        return pl.BlockSpec(a.shape, lambda i: (0,) * nd)

    return pl.pallas_call(
        functools.partial(_combine_kernel, alpha=alpha), grid=(T // tm,),
        in_specs=[row(oc.shape[1]), row(os_.shape[1]), row(ow.shape[1]), row(LANES, ZF_NG // LANES),
                  row(od.shape[1]), row(om.shape[1]),
                  row(D, ZF_MG // D), row(D, ZF_MG // D + 1), row(D, ZF_MG // D + 2), row(D),
                  full(eg), full(wn), full(wd), full(wm), full(wo), full(g), full(b)],
        out_specs=[row(D), row(D)],
        out_shape=[jax.ShapeDtypeStruct((T, D), F32), jax.ShapeDtypeStruct((T, D), BF16)],
        compiler_params=_params(("parallel",)), name="mixer_combine")(
            oc, os_, ow, zf, od, om, zf, zf, zf, h, eg, wn, wd, wm, wo, g, b)


def _route(logits_t, rb):
    aff = jax.nn.sigmoid(logits_t)
    selv = aff + rb
    a_rows = [aff[e:e + 1] for e in range(N_EXPERTS)]
    s_rows = [selv[e:e + 1] for e in range(N_EXPERTS)]
    npg = EXPERTS_PER_GROUP
    best, grp = None, None
    for g in range(N_GROUPS):
        v = s_rows[g * npg:(g + 1) * npg]
        top2 = None
        for a in range(npg):
            for b in range(a + 1, npg):
                pair = v[a] + v[b]
                top2 = pair if top2 is None else jnp.maximum(top2, pair)
        if g == 0:
            best, grp = top2, jnp.zeros_like(top2, dtype=jnp.int32)
        else:
            better = top2 > best
            grp = jnp.where(better, g, grp)
            best = jnp.where(better, top2, best)

    def pick(rows, k):
        out = rows[k]
        for g in range(1, N_GROUPS):
            out = jnp.where(grp == g, rows[g * npg + k], out)
        return out

    v = [pick(s_rows, k) for k in range(npg)]
    a = [pick(a_rows, k) for k in range(npg)]
    b1, i1 = v[0], jnp.zeros_like(grp)
    for k in range(1, npg):
        gt = v[k] > b1
        i1 = jnp.where(gt, k, i1)
        b1 = jnp.where(gt, v[k], b1)
    b2, i2 = jnp.full_like(b1, -jnp.inf), jnp.zeros_like(grp)
    for k in range(npg):
        ok = (i1 != k) & (v[k] > b2)
        i2 = jnp.where(ok, k, i2)
        b2 = jnp.where(ok, v[k], b2)
    g1 = sum(jnp.where(i1 == k, a[k], 0.0) for k in range(npg))
    g2 = sum(jnp.where(i2 == k, a[k], 0.0) for k in range(npg))
    den = g1 + g2
    w1, w2 = g1 / den, g2 / den
    sub = lax.broadcasted_iota(jnp.int32, (LANES, logits_t.shape[1]), 0)
    comb = jnp.zeros(sub.shape, F32)
    for e in range(N_EXPERTS):
        g, k = divmod(e, npg)
        in_g = grp == g
        row = jnp.where(in_g & (i1 == k), w1, 0.0) + jnp.where(in_g & (i2 == k), w2, 0.0)
        comb = jnp.where(sub == e, row, comb)
    return comb


def _moe_kernel(x_ref, xb_ref, rwt_ref, rb_ref, w1_ref, w3_ref, w2_ref, g_ref, b_ref, of_ref, ob_ref,
                comb_ref, acc_ref, *, alpha, tm):
    e = pl.program_id(1)

    @pl.when(e == 0)
    def _routing():
        comb_ref[...] = _route(_dot_nt(rwt_ref[...], xb_ref[...]), rb_ref[...]).T
        acc_ref[...] = jnp.zeros(acc_ref.shape, F32)

    xb = xb_ref[...]
    lane = lax.broadcasted_iota(jnp.int32, (tm, LANES), 1)
    col = jnp.sum(jnp.where(lane == e, comb_ref[...], 0.0), -1, keepdims=True)
    hid = jax.nn.silu(_dot(xb, w1_ref[...])) * _dot(xb, w3_ref[...]) * col
    acc_ref[...] += _dot(hid.astype(BF16), w2_ref[...])

    @pl.when(e == N_EXPERTS - 1)
    def _finish():
        hn = _layer_norm(alpha * x_ref[...] + acc_ref[...], g_ref[...], b_ref[...])
        of_ref[...] = hn
        ob_ref[...] = hn.astype(BF16)


def _moe_call(h, hb, rwt, rb, w1, w3, w2, g, b, alpha, tm):
    T, D = h.shape
    F = w1.shape[2]
    row = pl.BlockSpec((tm, D), lambda i, e: (i, 0))

    def full(a):
        return pl.BlockSpec(a.shape, lambda i, e: (0, 0))

    return pl.pallas_call(
        functools.partial(_moe_kernel, alpha=alpha, tm=tm), grid=(T // tm, N_EXPERTS),
        in_specs=[row, row, full(rwt), full(rb),
                  pl.BlockSpec((None, D, F), lambda i, e: (e, 0, 0)),
                  pl.BlockSpec((None, D, F), lambda i, e: (e, 0, 0)),
                  pl.BlockSpec((None, F, D), lambda i, e: (e, 0, 0)), full(g), full(b)],
        out_specs=[row, row],
        out_shape=[jax.ShapeDtypeStruct((T, D), F32), jax.ShapeDtypeStruct((T, D), BF16)],
        scratch_shapes=[pltpu.VMEM((tm, LANES), F32), pltpu.VMEM((tm, D), F32)],
        compiler_params=_params(("parallel", "arbitrary")), name="moe")(h, hb, rwt, rb, w1, w3, w2, g, b)


def _head_pad_cols(w, n_heads, width, scale=1.0):
    K = w.shape[0]
    w = (w * scale).reshape(K, n_heads, width)
    return jnp.pad(w, ((0, 0), (0, 0), (0, LANES - width))).reshape(K, n_heads * LANES)


def _head_pad_rows(w, n_heads, width):
    N = w.shape[1]
    w = w.reshape(n_heads, width, N)
    return jnp.pad(w, ((0, 0), (0, LANES - width), (0, 0))).reshape(n_heads * LANES, N)


def _rot_half_cols(w):
    half = w.shape[1] // 2
    return jnp.concatenate([-w[:, half:], w[:, :half]], axis=1)


def _in_proj_weights(w_in):
    D = w_in.shape[0]
    widths = (NSA_HEADS * HEAD_DIM,) + (NSA_GROUPS * HEAD_DIM,) * 6 + (
        3 * NSA_HEADS, DIFF_HEADS * 2 * DIFF_DIM, DIFF_HEADS * 2 * DIFF_DIM, DIFF_HEADS * 2 * DIFF_DIM,
        Q_LORA, KV_LORA, MLA_ROPE, 3 * D)
    parts, o = [], 0
    for w in widths:
        parts.append(w_in[:, o:o + w])
        o += w
    nq, kc, vc, ks, vs, kw, vw, ng, dq, dk, dv, cq, ckv, kr, mg = parts
    att_scale = HEAD_DIM ** -0.5
    dq4 = (dq * (DIFF_DIM ** -0.5)).reshape(D, DIFF_HEADS, 2, DIFF_DIM)
    zeros = jnp.zeros_like(dq4[:, :, 0])
    dq_exp = jnp.stack([jnp.concatenate([dq4[:, :, 0], zeros], -1),
                        jnp.concatenate([zeros, dq4[:, :, 1]], -1)], axis=2).reshape(D, DIFF_HEADS * 2 * LANES)
    wb = jnp.concatenate([
        _head_pad_cols(nq, NSA_HEADS, HEAD_DIM, att_scale),
        _head_pad_cols(ks, NSA_GROUPS, HEAD_DIM), _head_pad_cols(vs, NSA_GROUPS, HEAD_DIM),
        _head_pad_cols(kw, NSA_GROUPS, HEAD_DIM), _head_pad_cols(vw, NSA_GROUPS, HEAD_DIM),
        dq_exp, dk, dv], axis=1).astype(BF16)

    def rope_block(w):
        return jnp.pad(w, ((0, 0), (MLA_NOPE, LANES - MLA_NOPE - MLA_ROPE)))

    wf = jnp.concatenate([
        cq, kc, vc, jnp.pad(ng, ((0, 0), (0, LANES - ng.shape[1]))), ckv,
        rope_block(kr), rope_block(_rot_half_cols(kr)), mg], axis=1).astype(BF16)
    assert wb.shape[1] == ZB_WIDTH and wf.shape[1] == ZF_WIDTH
    return wb, wf


def _rope_tables(S):
    half = MLA_ROPE // 2
    freqs = ROPE_THETA ** (-jnp.arange(half, dtype=F32) / half)
    ang = jnp.arange(S, dtype=F32)[:, None] * freqs[None, :]
    cos = jnp.concatenate([jnp.cos(ang), jnp.cos(ang)], -1)
    sin = jnp.concatenate([jnp.sin(ang), jnp.sin(ang)], -1)
    tail = jnp.zeros((S, LANES - MLA_NOPE - MLA_ROPE), F32)
    cos_q = jnp.concatenate([jnp.ones((S, MLA_NOPE), F32), cos, tail], -1)
    cos_k = jnp.concatenate([jnp.zeros((S, MLA_NOPE), F32), cos, tail], -1)
    sin_qk = jnp.concatenate([jnp.zeros((S, MLA_NOPE), F32), sin, tail], -1)
    return cos_q, sin_qk, cos_k, sin_qk


def _score_matrix_t(S):
    nch = S // CMP_STRIDE
    ratio = CMP_LEN // CMP_STRIDE
    per_sb = SEL_BLOCK // CMP_STRIDE
    sb = jnp.arange(LANES)[:, None]
    cb = jnp.arange(nch)[None, :]
    m = jnp.zeros((LANES, nch), F32)
    for jj in range(ratio):
        chunk = cb + jj
        m = m + ((chunk // per_sb == sb) & (chunk < nch)).astype(F32)
    return m.astype(BF16)


def _cmp_flat(z, B, S):
    nch = S // CMP_STRIDE
    x = z.reshape(B, S, NSA_GROUPS, HEAD_DIM).transpose(0, 2, 1, 3).reshape(B, NSA_GROUPS, nch, CMP_STRIDE * HEAD_DIM)
    nxt = jnp.roll(x, -1, axis=2)
    return jnp.concatenate([x, nxt], -1).reshape(B * NSA_GROUPS * nch, CMP_LEN * HEAD_DIM)


def _mixer_and_ffn(h, hb, layer, B, S, p, shared, alpha):
    T, D = h.shape
    tq = min(WINDOW, S)
    wb, wf = _in_proj_weights(p["w_in"])
    zb = _matmul(hb, wb, BF16, min(1024, T), 512, "in_proj_b")
    zf = _matmul(hb, wf, F32, min(1024, T), 512, "in_proj_f")
    zb3 = zb.reshape(B, S, ZB_WIDTH)
    slopes = _alibi_slopes()
    nsa_slopes, diff_slopes = slopes[:NSA_HEADS], slopes[NSA_HEADS:]

    nch = S // CMP_STRIDE
    cmp_out = []
    for off, pos, w1, w2 in ((ZF_KC, p["cmp_pos_k"], p["cmp_w1_k"], p["cmp_w2_k"]),
                             (ZF_VC, p["cmp_pos_v"], p["cmp_w1_v"], p["cmp_w2_v"])):
        flat = _cmp_flat(zf[:, off:off + NSA_GROUPS * HEAD_DIM], B, S)
        w2p = jnp.pad(w2, ((0, 0), (0, LANES - HEAD_DIM))).astype(BF16)
        out = _compress_call(flat, pos.reshape(1, CMP_LEN * HEAD_DIM), w1.astype(BF16), w2p,
                             min(512, flat.shape[0]), "nsa_compress")
        cmp_out.append(out.reshape(B, NSA_GROUPS, nch, LANES))
    o_cmp, sel = _cmp_call(zb3, cmp_out[0], cmp_out[1], shared["score_t"], nsa_slopes, min(256, S))
    grp_map = [h_ // NSA_HPG for h_ in range(NSA_HEADS)]
    o_sel = _flash_call(zb3, ZB_QN, NSA_HEADS, zb3, ZB_KS, NSA_GROUPS, zb3, ZB_VS, NSA_GROUPS, shared["rel"],
                        kmap=grp_map, vmap=grp_map, slopes=nsa_slopes, mode="causal", tq=tq, name="nsa_sel",
                        sel=sel, emat=shared["emat"])
    o_win = _flash_call(zb3, ZB_QN, NSA_HEADS, zb3, ZB_KW, NSA_GROUPS, zb3, ZB_VW, NSA_GROUPS, shared["rel"],
                        kmap=grp_map, vmap=grp_map, slopes=nsa_slopes, mode="window", tq=tq, name="nsa_win")

    lam_init = 0.8 - 0.6 * math.exp(-0.3 * layer)
    pair_map = [h_ // 2 for h_ in range(2 * DIFF_HEADS)]
    o_diff = _flash_call(zb3, ZB_DQ, 2 * DIFF_HEADS, zb3, ZB_DK, DIFF_HEADS, zb3, ZB_DV, DIFF_HEADS, shared["rel"],
                         kmap=pair_map, vmap=pair_map, slopes=[diff_slopes[h_ // 2] for h_ in range(2 * DIFF_HEADS)],
                         mode="causal", tq=tq, name="diff_attn", lam=p["diff_lambda"],
                         subg=p["diff_subln_g"].reshape(1, 2 * DIFF_DIM), lam_init=lam_init)

    w_uq = p["mla_w_uq"].reshape(Q_LORA, MLA_HEADS, MLA_NOPE + MLA_ROPE)
    wq = jnp.pad(w_uq, ((0, 0), (0, 0), (0, LANES - MLA_NOPE - MLA_ROPE))).reshape(Q_LORA, MLA_HEADS * LANES)
    rot = jnp.stack([_rot_half_cols(w_uq[:, h_, MLA_NOPE:]) for h_ in range(MLA_HEADS)], axis=1)
    wqr = jnp.pad(rot, ((0, 0), (0, 0), (MLA_NOPE, LANES - MLA_NOPE - MLA_ROPE))).reshape(Q_LORA, MLA_HEADS * LANES)
    w_ukv = p["mla_w_ukv"].reshape(KV_LORA, MLA_HEADS, MLA_NOPE + MLA_V)
    wk = _head_pad_cols(w_ukv[:, :, :MLA_NOPE].reshape(KV_LORA, -1), MLA_HEADS, MLA_NOPE)
    wv = _head_pad_cols(w_ukv[:, :, MLA_NOPE:].reshape(KV_LORA, -1), MLA_HEADS, MLA_V)
    qm, km, vm = _mla_prep_call(zf, shared["rope"], p["mla_q_norm_g"].reshape(1, Q_LORA),
                                p["mla_kv_norm_g"].reshape(1, KV_LORA), wq.astype(BF16), wqr.astype(BF16),
                                wk.astype(BF16), wv.astype(BF16), S, min(512, S))
    ident = list(range(MLA_HEADS))
    o_mla = _flash_call(qm.reshape(B, S, -1), 0, MLA_HEADS, km.reshape(B, S, -1), 0, MLA_HEADS,
                        vm.reshape(B, S, -1), 0, MLA_HEADS, shared["rel"], kmap=ident, vmap=ident,
                        slopes=[0.0] * MLA_HEADS, mode="causal", tq=tq, name="mla_attn")

    h1, h1b = _combine_call(
        o_cmp.reshape(T, -1), o_sel.reshape(T, -1), o_win.reshape(T, -1), zf, o_diff.reshape(T, -1),
        o_mla.reshape(T, -1), h, shared["gate_expand"],
        _head_pad_rows(p["w_br_nsa"], NSA_HEADS, HEAD_DIM).astype(BF16), p["w_br_diff"].astype(BF16),
        _head_pad_rows(p["w_br_mla"], MLA_HEADS, MLA_V).astype(BF16), p["w_out"].astype(BF16),
        p["ln1_g"].reshape(1, D), p["ln1_b"].reshape(1, D), alpha, min(256, T))

    return _moe_call(h1, h1b, shared["router_wt"], shared["router_b"], p["moe_w1"].astype(BF16),
                     p["moe_w3"].astype(BF16), p["moe_w2"].astype(BF16), p["ln2_g"].reshape(1, D),
                     p["ln2_b"].reshape(1, D), alpha, min(512, T))


def kernel(x, ln_in_g, ln_in_b, w_in, cmp_pos_k, cmp_w1_k, cmp_w2_k, cmp_pos_v, cmp_w1_v, cmp_w2_v, diff_lambda, diff_subln_g, mla_q_norm_g, mla_kv_norm_g, mla_w_uq, mla_w_ukv, w_br_nsa, w_br_diff, w_br_mla, w_out, ln1_g, ln1_b, router_w, router_b, moe_w1, moe_w3, moe_w2, ln2_g, ln2_b):
    B, S, D = x.shape
    depth = w_in.shape[0]
    alpha = (2 * depth) ** 0.25
    T = B * S
    tq = min(WINDOW, S)
    idx = jnp.arange(tq, dtype=jnp.int32)
    gate_rows = jnp.arange(LANES)[:, None]
    gate_cols = jnp.arange(NSA_HEADS * LANES)[None, :] // LANES
    shared = {
        "rel": (idx[None, :] - idx[:, None]).astype(F32),
        "emat": (jnp.arange(LANES)[:, None] == (jnp.arange(S)[None, :] // SEL_BLOCK)).astype(BF16),
        "score_t": _score_matrix_t(S),
        "rope": _rope_tables(S),
        "gate_expand": jnp.stack([(gate_rows == gate_cols * 3 + j) for j in range(3)]).astype(BF16),
        "router_wt": router_w.T.astype(BF16),
        "router_b": router_b.reshape(N_EXPERTS, 1).astype(F32),
    }
    per_layer = dict(w_in=w_in, cmp_pos_k=cmp_pos_k, cmp_w1_k=cmp_w1_k, cmp_w2_k=cmp_w2_k, cmp_pos_v=cmp_pos_v,
                     cmp_w1_v=cmp_w1_v, cmp_w2_v=cmp_w2_v, diff_lambda=diff_lambda, diff_subln_g=diff_subln_g,
                     mla_q_norm_g=mla_q_norm_g, mla_kv_norm_g=mla_kv_norm_g, mla_w_uq=mla_w_uq, mla_w_ukv=mla_w_ukv,
                     w_br_nsa=w_br_nsa, w_br_diff=w_br_diff, w_br_mla=w_br_mla, w_out=w_out, ln1_g=ln1_g,
                     ln1_b=ln1_b, moe_w1=moe_w1, moe_w3=moe_w3, moe_w2=moe_w2, ln2_g=ln2_g, ln2_b=ln2_b)
    h, hb = _ln_call(x.reshape(T, D), ln_in_g, ln_in_b, min(512, T))
    for l in range(depth):
        p = {k: v[l] for k, v in per_layer.items()}
        h, hb = _mixer_and_ffn(h, hb, l, B, S, p, shared, alpha)
    return h.reshape(B, S, D)
```

```python
import functools
import math

import numpy as np
import jax
import jax.numpy as jnp
from jax import lax
from jax.experimental import pallas as pl
from jax.experimental.pallas import tpu as pltpu

F32 = jnp.float32
BF16 = jnp.bfloat16

LANES = 128
HEAD_DIM = 64
NSA_HEADS = 8
NSA_GROUPS = 2
NSA_HPG = NSA_HEADS // NSA_GROUPS
CMP_LEN = 32
CMP_STRIDE = 16
CMP_HIDDEN = 256
SEL_BLOCK = 64
N_SELECT = 16
WINDOW = 512
FORCE_BONUS = 1.0e4
DIFF_HEADS = 4
DIFF_DIM = 64
MLA_HEADS = 8
MLA_NOPE = 64
MLA_ROPE = 32
MLA_V = 64
Q_LORA = 256
KV_LORA = 128
ROPE_THETA = 10000.0
N_EXPERTS = 16
N_GROUPS = 4
EXPERTS_PER_GROUP = N_EXPERTS // N_GROUPS
D_FF_EXPERT = 512
LN_EPS = 1e-5
RMS_EPS = 1e-6
NEG = -1e30
LOG2E = math.log2(math.e)
MASK_BIG = 2.0 ** 100
AUX_LANE = HEAD_DIM
ALIBI_TERMS = 3
ALIBI_RADIX = 16
QK_LOOKAHEAD = 2
STEP_ACTIVE, STEP_FIRST, STEP_DIAG, STEP_GROUP0 = 1, 2, 4, 8

VMEM_LIMIT = 56 * 1024 * 1024

ZB_QN, ZB_DQ, ZB_DK, ZB_KS, ZB_VS, ZB_KW, ZB_VW, ZB_DV, ZB_WIDTH = 0, 1024, 2048, 3072, 3328, 3584, 3840, 4096, 4608
ZF_CQ, ZF_KC, ZF_VC, ZF_NG, ZF_CKV, ZF_KR, ZF_KRR, ZF_MG, ZF_WIDTH = 0, 256, 384, 512, 640, 768, 896, 1024, 4096


def _alibi_slopes_log2():
    n = NSA_HEADS + DIFF_HEADS
    return [LOG2E * 2.0 ** (-8.0 * i / n) for i in range(1, n + 1)]


def _params(sem):
    return pltpu.CompilerParams(dimension_semantics=sem, vmem_limit_bytes=VMEM_LIMIT)


def _layer_norm(x, g, b):
    mu = jnp.mean(x, -1, keepdims=True)
    xc = x - mu
    var = jnp.mean(xc * xc, -1, keepdims=True)
    return xc * lax.rsqrt(var + LN_EPS) * g + b


def _rms_norm(x, g):
    return x * lax.rsqrt(jnp.mean(x * x, -1, keepdims=True) + RMS_EPS) * g


def _dot(a, b):
    return jnp.dot(a, b, preferred_element_type=F32)


def _dot_nt(a, b):
    return lax.dot_general(a, b, (((1,), (1,)), ((), ())), preferred_element_type=F32)


def _ln_kernel(x_ref, g_ref, b_ref, of_ref, ob_ref):
    y = _layer_norm(x_ref[...], g_ref[...], b_ref[...])
    of_ref[...] = y
    ob_ref[...] = y.astype(BF16)


def _ln_call(x, g, b, tm):
    T, D = x.shape
    row = pl.BlockSpec((tm, D), lambda i: (i, 0))
    vec = pl.BlockSpec((1, D), lambda i: (0, 0))
    return pl.pallas_call(
        _ln_kernel, grid=(T // tm,), in_specs=[row, vec, vec], out_specs=[row, row],
        out_shape=[jax.ShapeDtypeStruct((T, D), F32), jax.ShapeDtypeStruct((T, D), BF16)],
        compiler_params=_params(("parallel",)), name="ln_in")(x, g.reshape(1, D), b.reshape(1, D))


def _mm_kernel(a_ref, w_ref, o_ref):
    o_ref[...] = _dot(a_ref[...], w_ref[...]).astype(o_ref.dtype)


def _mm_const_kernel(a_ref, w_ref, c_ref, o_ref):
    o_ref[...] = (_dot(a_ref[...], w_ref[...]) + c_ref[...]).astype(o_ref.dtype)


def _matmul(a, w, out_dtype, tm, tn, name, const_row=None):
    M, K = a.shape
    N = w.shape[1]
    in_specs = [pl.BlockSpec((tm, K), lambda i, j: (i, 0)), pl.BlockSpec((K, tn), lambda i, j: (0, j))]
    args = [a, w]
    if const_row is not None:
        in_specs.append(pl.BlockSpec((1, tn), lambda i, j: (0, j)))
        args.append(const_row)
    return pl.pallas_call(
        _mm_kernel if const_row is None else _mm_const_kernel, grid=(M // tm, N // tn), in_specs=in_specs,
        out_specs=pl.BlockSpec((tm, tn), lambda i, j: (i, j)),
        out_shape=jax.ShapeDtypeStruct((M, N), out_dtype),
        compiler_params=_params(("parallel", "arbitrary")), name=name)(*args)


def _compress_kernel(x_ref, pos_ref, w1_ref, w2_ref, o_ref):
    flat = (x_ref[...] + pos_ref[...]).astype(BF16)
    hid = jax.nn.gelu(_dot(flat, w1_ref[...]))
    o_ref[...] = _dot(hid.astype(BF16), w2_ref[...]).astype(o_ref.dtype)


def _compress_call(flat, pos_flat, w1, w2p, tm, name):
    R, W = flat.shape
    return pl.pallas_call(
        _compress_kernel, grid=(R // tm,),
        in_specs=[pl.BlockSpec((tm, W), lambda i: (i, 0)), pl.BlockSpec((1, W), lambda i: (0, 0)),
                  pl.BlockSpec(w1.shape, lambda i: (0, 0)), pl.BlockSpec(w2p.shape, lambda i: (0, 0))],
        out_specs=pl.BlockSpec((tm, LANES), lambda i: (i, 0)),
        out_shape=jax.ShapeDtypeStruct((R, LANES), BF16),
        compiler_params=_params(("parallel",)), name=name)(flat, pos_flat, w1, w2p)


def _mla_prep_kernel(cq_ref, ckv_ref, kr_ref, krr_ref, cq128_ref, sq128_ref, ck128_ref, sk128_ref,
                     qg_ref, kvg_ref, wq_ref, wqr_ref, wk_ref, wv_ref, q_out, k_out, v_out, *, scale):
    cqn = _rms_norm(cq_ref[...], qg_ref[...]).astype(BF16)
    q_main = _dot(cqn, wq_ref[...])
    q_rot = _dot(cqn, wqr_ref[...])
    ckvn = _rms_norm(ckv_ref[...], kvg_ref[...]).astype(BF16)
    k_nope = _dot(ckvn, wk_ref[...])
    lane = lax.broadcasted_iota(jnp.int32, (1, MLA_HEADS * LANES), 1)
    sum_lane = ((lane & (LANES - 1)) == AUX_LANE).astype(F32)
    v_out[...] = (_dot(ckvn, wv_ref[...]) + sum_lane).astype(BF16)
    k_rope = kr_ref[...] * ck128_ref[...] + krr_ref[...] * sk128_ref[...]
    cq128 = cq128_ref[...]
    sq128 = sq128_ref[...]
    for h in range(MLA_HEADS):
        sl = slice(h * LANES, (h + 1) * LANES)
        q_out[:, sl] = ((q_main[:, sl] * cq128 + q_rot[:, sl] * sq128) * scale).astype(BF16)
        k_out[:, sl] = (k_nope[:, sl] + k_rope).astype(BF16)


def _mla_prep_call(zf, tabs, qg, kvg, wq, wqr, wk, wv, S, tm):
    T = zf.shape[0]
    npos = S // tm
    HW = MLA_HEADS * LANES

    def col(width, off):
        return pl.BlockSpec((tm, width), lambda i: (i, off // width))

    tab = pl.BlockSpec((tm, LANES), lambda i: (i % npos, 0))

    def full(a):
        return pl.BlockSpec(a.shape, lambda i: (0, 0))

    out = pl.BlockSpec((tm, HW), lambda i: (i, 0))
    return pl.pallas_call(
        functools.partial(_mla_prep_kernel, scale=LOG2E * (MLA_NOPE + MLA_ROPE) ** -0.5), grid=(T // tm,),
        in_specs=[col(Q_LORA, ZF_CQ), col(KV_LORA, ZF_CKV), col(LANES, ZF_KR), col(LANES, ZF_KRR),
                  tab, tab, tab, tab, full(qg), full(kvg), full(wq), full(wqr), full(wk), full(wv)],
        out_specs=[out, out, out],
        out_shape=[jax.ShapeDtypeStruct((T, HW), BF16)] * 3,
        compiler_params=_params(("parallel",)), name="mla_prep")(
            zf, zf, zf, zf, *tabs, qg, kvg, wq, wqr, wk, wv)


def _cmp_kernel(q_ref, kc_ref, vc_ref, mt_ref, o_ref, sel_ref, any_ref, *, slopes, tq, nch, nsb, nsel):
    i = pl.program_id(1)
    t0 = i * tq
    r = lax.broadcasted_iota(jnp.int32, (tq, nch), 0)
    c = lax.broadcasted_iota(jnp.int32, (tq, nch), 1)
    dist = t0 + r - (CMP_STRIDE * c + (CMP_LEN - 1))
    valid = (dist >= 0) & (c < nch - 1)
    dist_f = dist.astype(F32)
    sb = lax.broadcasted_iota(jnp.int32, (LANES, tq), 0)
    sb_f = sb.astype(F32)
    t = t0 + lax.broadcasted_iota(jnp.int32, (LANES, tq), 1)
    cur = lax.shift_right_arithmetic(t, SEL_BLOCK.bit_length() - 1)
    forced = (sb == 0) | (sb == cur) | (sb == cur - 1)
    started = sb * SEL_BLOCK <= t
    mt = mt_ref[...]
    for g in range(NSA_GROUPS):
        kc = kc_ref[0, g]
        vc = vc_ref[0, g]
        p_grp = jnp.zeros((tq, nch), F32)
        for hh in range(NSA_HPG):
            h = g * NSA_HPG + hh
            sl = slice(h * LANES, (h + 1) * LANES)
            s = _dot_nt(q_ref[0, :, sl], kc)
            z = jnp.where(valid, s - slopes[h] * dist_f, NEG)
            mx = jnp.max(z, -1, keepdims=True)
            e = jnp.where(valid, jnp.exp2(z - mx), 0.0)
            p = e / jnp.maximum(jnp.sum(e, -1, keepdims=True), 1e-30)
            o_ref[0, :, sl] = _dot(p.astype(BF16), vc).astype(BF16)
            p_grp = p_grp + p
        p1 = p_grp.astype(BF16)
        r1 = p_grp - p1.astype(F32)
        p2 = r1.astype(BF16)
        p3 = (r1 - p2.astype(F32)).astype(BF16)
        sc = _dot_nt(mt, p1) + _dot_nt(mt, p2) + _dot_nt(mt, p3)
        sc = jnp.where(forced, sc + FORCE_BONUS, sc)
        sc = jnp.where(started, sc, -FORCE_BONUS)
        sc = jnp.where(sb < nsb, sc, -jnp.inf)
        sel_t = jnp.zeros((LANES, tq), F32)
        for _ in range(nsel):
            m = jnp.max(sc, axis=0, keepdims=True)
            cand = jnp.where(sc == m, sb_f, float(LANES))
            idx = jnp.min(cand, axis=0, keepdims=True)
            hit = sb_f == idx
            sel_t = jnp.where(hit, 1.0, sel_t)
            sc = jnp.where(hit, -jnp.inf, sc)
        sel_mat = sel_t.T
        sel_ref[0, g] = ((sel_mat - 1.0) * MASK_BIG).astype(BF16)
        any_ref[0, 0, g:g + 1, :] = jnp.max(sel_mat, axis=0, keepdims=True)


def _cmp_call(zb3, kcmp, vcmp, mt, slopes, tq):
    B, S, _ = zb3.shape
    nch = S // CMP_STRIDE
    nsb = S // SEL_BLOCK
    HW = NSA_HEADS * LANES
    kern = functools.partial(_cmp_kernel, slopes=slopes, tq=tq, nch=nch, nsb=nsb, nsel=min(N_SELECT, nsb))
    cmp_spec = pl.BlockSpec((1, NSA_GROUPS, nch, LANES), lambda b, i: (b, 0, 0, 0))
    return pl.pallas_call(
        kern, grid=(B, S // tq),
        in_specs=[pl.BlockSpec((1, tq, HW), lambda b, i: (b, i, ZB_QN // HW)), cmp_spec, cmp_spec,
                  pl.BlockSpec(mt.shape, lambda b, i: (0, 0))],
        out_specs=[pl.BlockSpec((1, tq, HW), lambda b, i: (b, i, 0)),
                   pl.BlockSpec((1, NSA_GROUPS, tq, LANES), lambda b, i: (b, 0, i, 0)),
                   pl.BlockSpec((1, 1, NSA_GROUPS, LANES), lambda b, i: (b, i, 0, 0))],
        out_shape=[jax.ShapeDtypeStruct((B, S, HW), BF16),
                   jax.ShapeDtypeStruct((B, NSA_GROUPS, S, LANES), BF16),
                   jax.ShapeDtypeStruct((B, S // tq, NSA_GROUPS, LANES), F32)],
        compiler_params=_params(("parallel", "parallel")), name="nsa_cmp")(zb3, kcmp, vcmp, mt)


def _flash_kernel(*refs, n_heads, kmap, vmap, slopes, mode, has_sel, sum_lane, finalize, tq, tk, lam_init):
    kj_ref, fl_ref = refs[1], refs[2]
    it = iter(refs[3:])
    q_ref, k_ref, v_ref, rel_ref, ktab_ref = next(it), next(it), next(it), next(it), next(it)
    if has_sel:
        sel_ref, et_ref = next(it), next(it)
    if finalize == "diff":
        lam_ref, subg_ref = next(it), next(it)
    o_ref, m_ref, acc_ref = next(it), next(it), next(it)
    l_ref = None if sum_lane else next(it)

    b = pl.program_id(0)
    s = pl.program_id(1)
    kv = kj_ref[b, s]
    flags = fl_ref[b, s]
    is_diag = (flags & STEP_DIAG) != 0
    is_off = ((flags & STEP_ACTIVE) != 0) & jnp.logical_not(is_diag)

    @pl.when((flags & STEP_FIRST) != 0)
    def _init():
        m_ref[...] = jnp.full(m_ref.shape, NEG, F32)
        acc_ref[...] = jnp.zeros(acc_ref.shape, F32)
        if l_ref is not None:
            l_ref[...] = jnp.zeros(l_ref.shape, F32)

    def run_heads(heads, diag):
        if diag:
            mask = rel_ref[...] <= 0
        elif mode == "window":
            mask = rel_ref[...] > 0
        else:
            mask = None
        key0 = (kv * tk).astype(F32)
        k_blocks = {}

        def scores(h):
            kb = kmap[h]
            if kb not in k_blocks:
                k = k_ref[0, :, kb * LANES:(kb + 1) * LANES]
                if slopes[h] != 0.0:
                    k = k + ktab_ref[...]
                if has_sel:
                    k = jnp.concatenate([k, et_ref[...]], axis=1)
                k_blocks[kb] = k
            q = q_ref[0, :, h * LANES:(h + 1) * LANES]
            if has_sel:
                q = jnp.concatenate([q, sel_ref[0, h // NSA_HPG]], axis=1)
            return _dot_nt(q, k_blocks[kb])

        pending = [scores(h) for h in heads[:QK_LOOKAHEAD]]
        for n, h in enumerate(heads):
            u = pending.pop(0)
            if n + QK_LOOKAHEAD < len(heads):
                pending.append(scores(heads[n + QK_LOOKAHEAD]))
            v = v_ref[0, :, vmap[h] * LANES:(vmap[h] + 1) * LANES]
            delta = slopes[h] * key0 if slopes[h] != 0.0 else 0.0
            if mask is not None:
                u = jnp.where(mask, u, NEG)
            m_prev = m_ref[h]
            m_new = jnp.maximum(m_prev, jnp.max(u, -1, keepdims=True) + delta)
            alpha = jnp.exp2(m_prev - m_new)
            shift = m_new - delta
            psum = None
            chunks = []
            for c in range(tk // LANES):
                pc = jnp.exp2(u[:, c * LANES:(c + 1) * LANES] - shift)
                if l_ref is not None:
                    psum = pc if psum is None else psum + pc
                chunks.append(pc.astype(BF16))
            p = jnp.concatenate(chunks, axis=1)
            if l_ref is not None:
                l_ref[h] = alpha * l_ref[h] + psum
            acc_ref[h] = alpha * acc_ref[h] + _dot(p, v)
            m_ref[h] = m_new

    def normalised(h):
        acc = acc_ref[h]
        if l_ref is None:
            l = acc[:, AUX_LANE:AUX_LANE + 1]
        else:
            l = jnp.sum(l_ref[h], -1, keepdims=True)
        return acc * (1.0 / jnp.maximum(l, 1e-30))

    all_heads = list(range(n_heads))

    @pl.when(is_off)
    def _off():
        if has_sel:
            for g in range(NSA_GROUPS):
                @pl.when((flags & (STEP_GROUP0 << g)) != 0)
                def _group():
                    run_heads(all_heads[g * NSA_HPG:(g + 1) * NSA_HPG], False)
        else:
            run_heads(all_heads, False)

    @pl.when(is_diag)
    def _diag():
        run_heads(all_heads, True)
        if finalize == "plain":
            head_lanes = lax.broadcasted_iota(jnp.int32, (tq, LANES), 1) < HEAD_DIM
            for h in range(n_heads):
                o_ref[0, :, h * LANES:(h + 1) * LANES] = jnp.where(head_lanes, normalised(h), 0.0).astype(o_ref.dtype)
        else:
            lp = lam_ref[...]
            lam = (jnp.exp(jnp.sum(lp[0:1] * lp[1:2], -1, keepdims=True))
                   - jnp.exp(jnp.sum(lp[2:3] * lp[3:4], -1, keepdims=True)) + lam_init)
            for hd in range(n_heads // 2):
                o = normalised(2 * hd) - lam * normalised(2 * hd + 1)
                o = _rms_norm(o, subg_ref[...]) * (1.0 - lam_init)
                o_ref[0, :, hd * LANES:(hd + 1) * LANES] = o.astype(o_ref.dtype)


def _static_steps(B, nq, mode):
    qi, kj, fl = [], [], []
    for i in range(nq):
        first_j = 0 if mode == "causal" else max(i - 1, 0)
        for j in range(first_j, i + 1):
            qi.append(i)
            kj.append(j)
            fl.append(STEP_ACTIVE | (STEP_FIRST if j == first_j else 0) | (STEP_DIAG if j == i else 0))
    tile = lambda a: jnp.tile(jnp.asarray(a, jnp.int32)[None], (B, 1))
    return tile(qi), tile(kj), tile(fl)


def _selected_steps(any_sel, S, tq):
    B, nqc, G, _ = any_sel.shape
    nq = nk = S // tq
    per_tile = tq // SEL_BLOCK
    a = any_sel.reshape(B, nq, nqc // nq, G, LANES).max(axis=2) > 0
    a = a[..., :nk * per_tile].reshape(B, nq, G, nk, per_tile).any(-1)
    ii = jnp.arange(nq)[:, None]
    jj = jnp.arange(nk)[None, :]
    g_act = (a & (jj <= ii)[None, :, None, :]) | (ii == jj)[None, :, None, :]
    act = g_act.any(2)
    n_steps = nq * (nq + 1) // 2
    key = jnp.where(act, (ii * nk + jj)[None], nq * nk).reshape(B, nq * nk)
    order = jnp.sort(key, axis=1)[:, :n_steps]
    valid = order < nq * nk
    order = jnp.where(valid, order, nq * nk - 1)
    qi, kj = order // nk, order % nk
    first = valid & (qi != jnp.concatenate([jnp.full((B, 1), -1, qi.dtype), qi[:, :-1]], axis=1))
    g_bits = jnp.take_along_axis(g_act.transpose(0, 2, 1, 3).reshape(B, G, nq * nk), order[:, None, :], axis=2)
    flags = valid * STEP_ACTIVE + first * STEP_FIRST + (valid & (qi == kj)) * STEP_DIAG
    for g in range(G):
        flags = flags + (valid & g_bits[:, g]) * (STEP_GROUP0 << g)
    return qi.astype(jnp.int32), kj.astype(jnp.int32), flags.astype(jnp.int32)


def _flash_call(q_arr, q_off, n_heads, k_arr, k_off, k_blocks, v_arr, v_off, v_blocks, shared, *, kmap, vmap, slopes,
                mode, tq, name, steps=None, sel=None, lam=None, subg=None, lam_init=0.0):
    B, S, _ = q_arr.shape
    tk = tq
    nq = S // tq
    QW, KW, VW = n_heads * LANES, k_blocks * LANES, v_blocks * LANES
    if mode == "window":
        assert tq == WINDOW
    if steps is None:
        steps = _static_steps(B, nq, mode)
    n_steps = steps[0].shape[1]
    finalize = "diff" if lam is not None else "plain"
    sum_lane = finalize == "plain"
    out_heads = n_heads // 2 if finalize == "diff" else n_heads
    in_specs = [pl.BlockSpec((1, tq, QW), lambda b, s, qi, kj, fl: (b, qi[b, s], q_off // QW)),
                pl.BlockSpec((1, tk, KW), lambda b, s, qi, kj, fl: (b, kj[b, s], k_off // KW)),
                pl.BlockSpec((1, tk, VW), lambda b, s, qi, kj, fl: (b, kj[b, s], v_off // VW)),
                pl.BlockSpec((tq, tk), lambda b, s, qi, kj, fl: (0, 0)),
                pl.BlockSpec((tk, LANES), lambda b, s, qi, kj, fl: (0, 0))]
    args = [q_arr, k_arr, v_arr, shared["rel"], shared["ktab"]]
    if sel is not None:
        in_specs += [pl.BlockSpec((1, NSA_GROUPS, tq, LANES), lambda b, s, qi, kj, fl: (b, 0, qi[b, s], 0)),
                     pl.BlockSpec((tk, LANES), lambda b, s, qi, kj, fl: (kj[b, s], 0))]
        args += [sel, shared["block_onehot"]]
    if finalize == "diff":
        in_specs += [pl.BlockSpec(lam.shape, lambda b, s, qi, kj, fl: (0, 0)),
                     pl.BlockSpec(subg.shape, lambda b, s, qi, kj, fl: (0, 0))]
        args += [lam, subg]
    kern = functools.partial(_flash_kernel, n_heads=n_heads, kmap=kmap, vmap=vmap, slopes=slopes, mode=mode,
                             has_sel=sel is not None, sum_lane=sum_lane, finalize=finalize, tq=tq, tk=tk,
                             lam_init=lam_init)
    stat = pltpu.VMEM((n_heads, tq, LANES), F32)
    grid_spec = pltpu.PrefetchScalarGridSpec(
        num_scalar_prefetch=3, grid=(B, n_steps), in_specs=in_specs,
        out_specs=pl.BlockSpec((1, tq, out_heads * LANES), lambda b, s, qi, kj, fl: (b, qi[b, s], 0)),
        scratch_shapes=[stat, stat] if sum_lane else [stat, stat, stat])
    return pl.pallas_call(
        kern, grid_spec=grid_spec, out_shape=jax.ShapeDtypeStruct((B, S, out_heads * LANES), BF16),
        compiler_params=_params(("parallel", "arbitrary")), name=name)(*steps, *args)


def _combine_kernel(oc_ref, os_ref, ow_ref, ng_ref, od_ref, om_ref, mg0_ref, mg1_ref, mg2_ref, h_ref,
                    eg_ref, wn_ref, wd_ref, wm_ref, wo_ref, g_ref, b_ref, of_ref, ob_ref, *, alpha):
    sg = jax.nn.sigmoid(ng_ref[...]).astype(BF16)
    o_nsa = (_dot(sg, eg_ref[0]) * oc_ref[...].astype(F32)
             + _dot(sg, eg_ref[1]) * os_ref[...].astype(F32)
             + _dot(sg, eg_ref[2]) * ow_ref[...].astype(F32))
    y = (jax.nn.sigmoid(mg0_ref[...]) * _dot(o_nsa.astype(BF16), wn_ref[...])
         + jax.nn.sigmoid(mg1_ref[...]) * _dot(od_ref[...], wd_ref[...])
         + jax.nn.sigmoid(mg2_ref[...]) * _dot(om_ref[...], wm_ref[...]))
    mix = _dot(y.astype(BF16), wo_ref[...])
    hn = _layer_norm(alpha * h_ref[...] + mix, g_ref[...], b_ref[...])
    of_ref[...] = hn
    ob_ref[...] = hn.astype(BF16)


def _combine_call(oc, os_, ow, zf, od, om, h, eg, wn, wd, wm, wo, g, b, alpha, tm):
    T, D = h.shape

    def row(width, blk=0):
        return pl.BlockSpec((tm, width), lambda i: (i, blk))

    def full(a):
        nd = a.ndim
        return pl.BlockSpec(a.shape, lambda i: (0,) * nd)

    return pl.pallas_call(
        functools.partial(_combine_kernel, alpha=alpha), grid=(T // tm,),
        in_specs=[row(oc.shape[1]), row(os_.shape[1]), row(ow.shape[1]), row(LANES, ZF_NG // LANES),
                  row(od.shape[1]), row(om.shape[1]),
                  row(D, ZF_MG // D), row(D, ZF_MG // D + 1), row(D, ZF_MG // D + 2), row(D),
                  full(eg), full(wn), full(wd), full(wm), full(wo), full(g), full(b)],
        out_specs=[row(D), row(D)],
        out_shape=[jax.ShapeDtypeStruct((T, D), F32), jax.ShapeDtypeStruct((T, D), BF16)],
        compiler_params=_params(("parallel",)), name="mixer_combine")(
            oc, os_, ow, zf, od, om, zf, zf, zf, h, eg, wn, wd, wm, wo, g, b)


def _route(logits_t, rb):
    aff = jax.nn.sigmoid(logits_t)
    selv = aff + rb
    a_rows = [aff[e:e + 1] for e in range(N_EXPERTS)]
    s_rows = [selv[e:e + 1] for e in range(N_EXPERTS)]
    npg = EXPERTS_PER_GROUP
    best, grp = None, None
    for g in range(N_GROUPS):
        v = s_rows[g * npg:(g + 1) * npg]
        top2 = None
        for a in range(npg):
            for b in range(a + 1, npg):
                pair = v[a] + v[b]
                top2 = pair if top2 is None else jnp.maximum(top2, pair)
        if g == 0:
            best, grp = top2, jnp.zeros_like(top2, dtype=jnp.int32)
        else:
            better = top2 > best
            grp = jnp.where(better, g, grp)
            best = jnp.where(better, top2, best)

    def pick(rows, k):
        out = rows[k]
        for g in range(1, N_GROUPS):
            out = jnp.where(grp == g, rows[g * npg + k], out)
        return out

    v = [pick(s_rows, k) for k in range(npg)]
    a = [pick(a_rows, k) for k in range(npg)]
    b1, i1 = v[0], jnp.zeros_like(grp)
    for k in range(1, npg):
        gt = v[k] > b1
        i1 = jnp.where(gt, k, i1)
        b1 = jnp.where(gt, v[k], b1)
    b2, i2 = jnp.full_like(b1, -jnp.inf), jnp.zeros_like(grp)
    for k in range(npg):
        ok = (i1 != k) & (v[k] > b2)
        i2 = jnp.where(ok, k, i2)
        b2 = jnp.where(ok, v[k], b2)
    g1 = sum(jnp.where(i1 == k, a[k], 0.0) for k in range(npg))
    g2 = sum(jnp.where(i2 == k, a[k], 0.0) for k in range(npg))
    den = g1 + g2
    w1, w2 = g1 / den, g2 / den
    sub = lax.broadcasted_iota(jnp.int32, (LANES, logits_t.shape[1]), 0)
    comb = jnp.zeros(sub.shape, F32)
    for e in range(N_EXPERTS):
        g, k = divmod(e, npg)
        in_g = grp == g
        row = jnp.where(in_g & (i1 == k), w1, 0.0) + jnp.where(in_g & (i2 == k), w2, 0.0)
        comb = jnp.where(sub == e, row, comb)
    return comb


def _moe_kernel(x_ref, xb_ref, rwt_ref, rb_ref, w1_ref, w3_ref, w2_ref, g_ref, b_ref, of_ref, ob_ref,
                comb_ref, acc_ref, *, alpha, tm):
    e = pl.program_id(1)

    @pl.when(e == 0)
    def _routing():
        comb_ref[...] = _route(_dot_nt(rwt_ref[...], xb_ref[...]), rb_ref[...]).T
        acc_ref[...] = jnp.zeros(acc_ref.shape, F32)

    xb = xb_ref[...]
    lane = lax.broadcasted_iota(jnp.int32, (tm, LANES), 1)
    col = jnp.sum(jnp.where(lane == e, comb_ref[...], 0.0), -1, keepdims=True)
    hid = jax.nn.silu(_dot(xb, w1_ref[...])) * _dot(xb, w3_ref[...]) * col
    acc_ref[...] += _dot(hid.astype(BF16), w2_ref[...])

    @pl.when(e == N_EXPERTS - 1)
    def _finish():
        hn = _layer_norm(alpha * x_ref[...] + acc_ref[...], g_ref[...], b_ref[...])
        of_ref[...] = hn
        ob_ref[...] = hn.astype(BF16)


def _moe_call(h, hb, rwt, rb, w1, w3, w2, g, b, alpha, tm):
    T, D = h.shape
    F = w1.shape[2]
    row = pl.BlockSpec((tm, D), lambda i, e: (i, 0))

    def full(a):
        return pl.BlockSpec(a.shape, lambda i, e: (0, 0))

    return pl.pallas_call(
        functools.partial(_moe_kernel, alpha=alpha, tm=tm), grid=(T // tm, N_EXPERTS),
        in_specs=[row, row, full(rwt), full(rb),
                  pl.BlockSpec((None, D, F), lambda i, e: (e, 0, 0)),
                  pl.BlockSpec((None, D, F), lambda i, e: (e, 0, 0)),
                  pl.BlockSpec((None, F, D), lambda i, e: (e, 0, 0)), full(g), full(b)],
        out_specs=[row, row],
        out_shape=[jax.ShapeDtypeStruct((T, D), F32), jax.ShapeDtypeStruct((T, D), BF16)],
        scratch_shapes=[pltpu.VMEM((tm, LANES), F32), pltpu.VMEM((tm, D), F32)],
        compiler_params=_params(("parallel", "arbitrary")), name="moe")(h, hb, rwt, rb, w1, w3, w2, g, b)


def _head_pad_cols(w, n_heads, width, scale=1.0):
    K = w.shape[0]
    w = (w * scale).reshape(K, n_heads, width)
    return jnp.pad(w, ((0, 0), (0, 0), (0, LANES - width))).reshape(K, n_heads * LANES)


def _head_pad_rows(w, n_heads, width):
    N = w.shape[1]
    w = w.reshape(n_heads, width, N)
    return jnp.pad(w, ((0, 0), (0, LANES - width), (0, 0))).reshape(n_heads * LANES, N)


def _rot_half_cols(w):
    half = w.shape[1] // 2
    return jnp.concatenate([-w[:, half:], w[:, :half]], axis=1)


def _in_proj_weights(w_in):
    D = w_in.shape[0]
    widths = (NSA_HEADS * HEAD_DIM,) + (NSA_GROUPS * HEAD_DIM,) * 6 + (
        3 * NSA_HEADS, DIFF_HEADS * 2 * DIFF_DIM, DIFF_HEADS * 2 * DIFF_DIM, DIFF_HEADS * 2 * DIFF_DIM,
        Q_LORA, KV_LORA, MLA_ROPE, 3 * D)
    parts, o = [], 0
    for w in widths:
        parts.append(w_in[:, o:o + w])
        o += w
    nq, kc, vc, ks, vs, kw, vw, ng, dq, dk, dv, cq, ckv, kr, mg = parts
    wb = jnp.concatenate([
        _head_pad_cols(nq, NSA_HEADS, HEAD_DIM, LOG2E * HEAD_DIM ** -0.5),
        _head_pad_cols(dq, 2 * DIFF_HEADS, DIFF_DIM, LOG2E * DIFF_DIM ** -0.5),
        _head_pad_cols(dk, 2 * DIFF_HEADS, DIFF_DIM),
        _head_pad_cols(ks, NSA_GROUPS, HEAD_DIM), _head_pad_cols(vs, NSA_GROUPS, HEAD_DIM),
        _head_pad_cols(kw, NSA_GROUPS, HEAD_DIM), _head_pad_cols(vw, NSA_GROUPS, HEAD_DIM),
        dv], axis=1).astype(BF16)

    def rope_block(w):
        return jnp.pad(w, ((0, 0), (MLA_NOPE, LANES - MLA_NOPE - MLA_ROPE)))

    wf = jnp.concatenate([
        cq, kc, vc, jnp.pad(ng, ((0, 0), (0, LANES - ng.shape[1]))), ckv,
        rope_block(kr), rope_block(_rot_half_cols(kr)), mg], axis=1).astype(BF16)
    assert wb.shape[1] == ZB_WIDTH and wf.shape[1] == ZF_WIDTH
    return wb, wf


def _bf16_terms(x, n):
    terms, rest = [], np.float32(x)
    for _ in range(n):
        t = np.float32(np.asarray(rest, dtype=BF16).astype(np.float32))
        terms.append(float(t))
        rest = np.float32(rest - t)
    return terms


def _aux_const_row(slopes):
    row = np.zeros((1, ZB_WIDTH), np.float32)
    q_blocks = [(ZB_QN + h * LANES, slopes[h]) for h in range(NSA_HEADS)]
    q_blocks += [(ZB_DQ + b * LANES, slopes[NSA_HEADS + b // 2]) for b in range(2 * DIFF_HEADS)]
    for off, slope in q_blocks:
        for n, term in enumerate(_bf16_terms(slope, ALIBI_TERMS)):
            row[0, off + AUX_LANE + 2 * n] = ALIBI_RADIX * term
            row[0, off + AUX_LANE + 2 * n + 1] = term
    for off in (ZB_VS, ZB_VW):
        for g in range(NSA_GROUPS):
            row[0, off + g * LANES + AUX_LANE] = 1.0
    return jnp.asarray(row)


def _key_offset_table(tk):
    tab = np.zeros((tk, LANES), np.float32)
    c = np.arange(tk)
    for n in range(ALIBI_TERMS):
        tab[:, AUX_LANE + 2 * n] = c // ALIBI_RADIX
        tab[:, AUX_LANE + 2 * n + 1] = c % ALIBI_RADIX
    return jnp.asarray(tab, BF16)


def _rope_tables(S):
    half = MLA_ROPE // 2
    freqs = ROPE_THETA ** (-jnp.arange(half, dtype=F32) / half)
    ang = jnp.arange(S, dtype=F32)[:, None] * freqs[None, :]
    cos = jnp.concatenate([jnp.cos(ang), jnp.cos(ang)], -1)
    sin = jnp.concatenate([jnp.sin(ang), jnp.sin(ang)], -1)
    tail = jnp.zeros((S, LANES - MLA_NOPE - MLA_ROPE), F32)
    cos_q = jnp.concatenate([jnp.ones((S, MLA_NOPE), F32), cos, tail], -1)
    cos_k = jnp.concatenate([jnp.zeros((S, MLA_NOPE), F32), cos, tail], -1)
    sin_qk = jnp.concatenate([jnp.zeros((S, MLA_NOPE), F32), sin, tail], -1)
    return cos_q, sin_qk, cos_k, sin_qk


def _score_matrix_t(S):
    nch = S // CMP_STRIDE
    ratio = CMP_LEN // CMP_STRIDE
    per_sb = SEL_BLOCK // CMP_STRIDE
    sb = jnp.arange(LANES)[:, None]
    cb = jnp.arange(nch)[None, :]
    m = jnp.zeros((LANES, nch), F32)
    for jj in range(ratio):
        chunk = cb + jj
        m = m + ((chunk // per_sb == sb) & (chunk < nch)).astype(F32)
    return m.astype(BF16)


def _cmp_flat(z, B, S):
    nch = S // CMP_STRIDE
    x = z.reshape(B, S, NSA_GROUPS, HEAD_DIM).transpose(0, 2, 1, 3).reshape(B, NSA_GROUPS, nch, CMP_STRIDE * HEAD_DIM)
    nxt = jnp.roll(x, -1, axis=2)
    return jnp.concatenate([x, nxt], -1).reshape(B * NSA_GROUPS * nch, CMP_LEN * HEAD_DIM)


def _mixer_and_ffn(h, hb, layer, B, S, p, shared, alpha):
    T, D = h.shape
    tq = min(WINDOW, S)
    wb, wf = _in_proj_weights(p["w_in"])
    zb = _matmul(hb, wb, BF16, min(1024, T), 512, "in_proj_b", const_row=shared["aux_row"])
    zf = _matmul(hb, wf, F32, min(1024, T), 512, "in_proj_f")
    zb3 = zb.reshape(B, S, ZB_WIDTH)
    slopes = _alibi_slopes_log2()
    nsa_slopes, diff_slopes = slopes[:NSA_HEADS], slopes[NSA_HEADS:]

    nch = S // CMP_STRIDE
    cmp_out = []
    for off, pos, w1, w2 in ((ZF_KC, p["cmp_pos_k"], p["cmp_w1_k"], p["cmp_w2_k"]),
                             (ZF_VC, p["cmp_pos_v"], p["cmp_w1_v"], p["cmp_w2_v"])):
        flat = _cmp_flat(zf[:, off:off + NSA_GROUPS * HEAD_DIM], B, S)
        w2p = jnp.pad(w2, ((0, 0), (0, LANES - HEAD_DIM))).astype(BF16)
        out = _compress_call(flat, pos.reshape(1, CMP_LEN * HEAD_DIM), w1.astype(BF16), w2p,
                             min(512, flat.shape[0]), "nsa_compress")
        cmp_out.append(out.reshape(B, NSA_GROUPS, nch, LANES))
    o_cmp, sel, any_sel = _cmp_call(zb3, cmp_out[0], cmp_out[1], shared["score_t"], nsa_slopes, min(256, S))
    grp_map = [h_ // NSA_HPG for h_ in range(NSA_HEADS)]
    o_sel = _flash_call(zb3, ZB_QN, NSA_HEADS, zb3, ZB_KS, NSA_GROUPS, zb3, ZB_VS, NSA_GROUPS, shared,
                        kmap=grp_map, vmap=grp_map, slopes=nsa_slopes, mode="causal", tq=tq, name="nsa_sel", sel=sel,
                        steps=_selected_steps(any_sel, S, tq))
    o_win = _flash_call(zb3, ZB_QN, NSA_HEADS, zb3, ZB_KW, NSA_GROUPS, zb3, ZB_VW, NSA_GROUPS, shared,
                        kmap=grp_map, vmap=grp_map, slopes=nsa_slopes, mode="window", tq=tq, name="nsa_win")

    lam_init = 0.8 - 0.6 * math.exp(-0.3 * layer)
    n_maps = 2 * DIFF_HEADS
    o_diff = _flash_call(zb3, ZB_DQ, n_maps, zb3, ZB_DK, n_maps, zb3, ZB_DV, DIFF_HEADS, shared,
                         kmap=list(range(n_maps)), vmap=[m_ // 2 for m_ in range(n_maps)],
                         slopes=[diff_slopes[m_ // 2] for m_ in range(n_maps)],
                         mode="causal", tq=tq, name="diff_attn", lam=p["diff_lambda"],
                         subg=p["diff_subln_g"].reshape(1, 2 * DIFF_DIM), lam_init=lam_init)

    w_uq = p["mla_w_uq"].reshape(Q_LORA, MLA_HEADS, MLA_NOPE + MLA_ROPE)
    wq = jnp.pad(w_uq, ((0, 0), (0, 0), (0, LANES - MLA_NOPE - MLA_ROPE))).reshape(Q_LORA, MLA_HEADS * LANES)
    rot = jnp.stack([_rot_half_cols(w_uq[:, h_, MLA_NOPE:]) for h_ in range(MLA_HEADS)], axis=1)
    wqr = jnp.pad(rot, ((0, 0), (0, 0), (MLA_NOPE, LANES - MLA_NOPE - MLA_ROPE))).reshape(Q_LORA, MLA_HEADS * LANES)
    w_ukv = p["mla_w_ukv"].reshape(KV_LORA, MLA_HEADS, MLA_NOPE + MLA_V)
    wk = _head_pad_cols(w_ukv[:, :, :MLA_NOPE].reshape(KV_LORA, -1), MLA_HEADS, MLA_NOPE)
    wv = _head_pad_cols(w_ukv[:, :, MLA_NOPE:].reshape(KV_LORA, -1), MLA_HEADS, MLA_V)
    qm, km, vm = _mla_prep_call(zf, shared["rope"], p["mla_q_norm_g"].reshape(1, Q_LORA),
                                p["mla_kv_norm_g"].reshape(1, KV_LORA), wq.astype(BF16), wqr.astype(BF16),
                                wk.astype(BF16), wv.astype(BF16), S, min(512, S))
    ident = list(range(MLA_HEADS))
    o_mla = _flash_call(qm.reshape(B, S, -1), 0, MLA_HEADS, km.reshape(B, S, -1), 0, MLA_HEADS,
                        vm.reshape(B, S, -1), 0, MLA_HEADS, shared, kmap=ident, vmap=ident,
                        slopes=[0.0] * MLA_HEADS, mode="causal", tq=tq, name="mla_attn")

    h1, h1b = _combine_call(
        o_cmp.reshape(T, -1), o_sel.reshape(T, -1), o_win.reshape(T, -1), zf, o_diff.reshape(T, -1),
        o_mla.reshape(T, -1), h, shared["gate_expand"],
        _head_pad_rows(p["w_br_nsa"], NSA_HEADS, HEAD_DIM).astype(BF16), p["w_br_diff"].astype(BF16),
        _head_pad_rows(p["w_br_mla"], MLA_HEADS, MLA_V).astype(BF16), p["w_out"].astype(BF16),
        p["ln1_g"].reshape(1, D), p["ln1_b"].reshape(1, D), alpha, min(256, T))

    return _moe_call(h1, h1b, shared["router_wt"], shared["router_b"], p["moe_w1"].astype(BF16),
                     p["moe_w3"].astype(BF16), p["moe_w2"].astype(BF16), p["ln2_g"].reshape(1, D),
                     p["ln2_b"].reshape(1, D), alpha, min(512, T))


def kernel(x, ln_in_g, ln_in_b, w_in, cmp_pos_k, cmp_w1_k, cmp_w2_k, cmp_pos_v, cmp_w1_v, cmp_w2_v, diff_lambda, diff_subln_g, mla_q_norm_g, mla_kv_norm_g, mla_w_uq, mla_w_ukv, w_br_nsa, w_br_diff, w_br_mla, w_out, ln1_g, ln1_b, router_w, router_b, moe_w1, moe_w3, moe_w2, ln2_g, ln2_b):
    B, S, D = x.shape
    depth = w_in.shape[0]
    alpha = (2 * depth) ** 0.25
    T = B * S
    tq = min(WINDOW, S)
    idx = jnp.arange(tq, dtype=jnp.int32)
    gate_rows = jnp.arange(LANES)[:, None]
    gate_cols = jnp.arange(NSA_HEADS * LANES)[None, :] // LANES
    shared = {
        "rel": (idx[None, :] - idx[:, None]).astype(F32),
        "ktab": _key_offset_table(tq),
        "aux_row": _aux_const_row(_alibi_slopes_log2()),
        "block_onehot": ((jnp.arange(S)[:, None] // SEL_BLOCK) == jnp.arange(LANES)[None, :]).astype(BF16),
        "score_t": _score_matrix_t(S),
        "rope": _rope_tables(S),
        "gate_expand": jnp.stack([(gate_rows == gate_cols * 3 + j) for j in range(3)]).astype(BF16),
        "router_wt": router_w.T.astype(BF16),
        "router_b": router_b.reshape(N_EXPERTS, 1).astype(F32),
    }
    per_layer = dict(w_in=w_in, cmp_pos_k=cmp_pos_k, cmp_w1_k=cmp_w1_k, cmp_w2_k=cmp_w2_k, cmp_pos_v=cmp_pos_v,
                     cmp_w1_v=cmp_w1_v, cmp_w2_v=cmp_w2_v, diff_lambda=diff_lambda, diff_subln_g=diff_subln_g,
                     mla_q_norm_g=mla_q_norm_g, mla_kv_norm_g=mla_kv_norm_g, mla_w_uq=mla_w_uq, mla_w_ukv=mla_w_ukv,
                     w_br_nsa=w_br_nsa, w_br_diff=w_br_diff, w_br_mla=w_br_mla, w_out=w_out, ln1_g=ln1_g,
                     ln1_b=ln1_b, moe_w1=moe_w1, moe_w3=moe_w3, moe_w2=moe_w2, ln2_g=ln2_g, ln2_b=ln2_b)
    h, hb = _ln_call(x.reshape(T, D), ln_in_g, ln_in_b, min(512, T))
    for l in range(depth):
        p = {k: v[l] for k, v in per_layer.items()}
        h, hb = _mixer_and_ffn(h, hb, l, B, S, p, shared, alpha)
    return h.reshape(B, S, D)
```

```python
import functools
import math

import numpy as np
import jax
import jax.numpy as jnp
from jax import lax
from jax.experimental import pallas as pl
from jax.experimental.pallas import tpu as pltpu

F32 = jnp.float32
BF16 = jnp.bfloat16

LANES = 128
HEAD_DIM = 64
NSA_HEADS = 8
NSA_GROUPS = 2
NSA_HPG = NSA_HEADS // NSA_GROUPS
CMP_LEN = 32
CMP_STRIDE = 16
CMP_HIDDEN = 256
SEL_BLOCK = 64
N_SELECT = 16
WINDOW = 512
FORCE_BONUS = 1.0e4
DIFF_HEADS = 4
DIFF_DIM = 64
MLA_HEADS = 8
MLA_NOPE = 64
MLA_ROPE = 32
MLA_V = 64
Q_LORA = 256
KV_LORA = 128
ROPE_THETA = 10000.0
N_EXPERTS = 16
N_GROUPS = 4
EXPERTS_PER_GROUP = N_EXPERTS // N_GROUPS
D_FF_EXPERT = 512
LN_EPS = 1e-5
RMS_EPS = 1e-6
NEG = -1e30
LOG2E = math.log2(math.e)
MASK_BIG = 2.0 ** 100
AUX_LANE = HEAD_DIM
ALIBI_TERMS = 3
ALIBI_RADIX = 16
QK_LOOKAHEAD = 2
STEP_ACTIVE, STEP_FIRST, STEP_DIAG, STEP_GROUP0 = 1, 2, 4, 8

VMEM_LIMIT = 56 * 1024 * 1024

ZB_QN, ZB_DQ, ZB_DK, ZB_KS, ZB_VS, ZB_KW, ZB_VW, ZB_DV, ZB_WIDTH = 0, 1024, 2048, 3072, 3328, 3584, 3840, 4096, 4608
ZF_CQ, ZF_KC, ZF_VC, ZF_NG, ZF_CKV, ZF_KR, ZF_KRR, ZF_MG, ZF_WIDTH = 0, 256, 384, 512, 640, 768, 896, 1024, 4096


def _alibi_slopes_log2():
    n = NSA_HEADS + DIFF_HEADS
    return [LOG2E * 2.0 ** (-8.0 * i / n) for i in range(1, n + 1)]


def _params(sem):
    return pltpu.CompilerParams(dimension_semantics=sem, vmem_limit_bytes=VMEM_LIMIT)


def _layer_norm(x, g, b):
    mu = jnp.mean(x, -1, keepdims=True)
    xc = x - mu
    var = jnp.mean(xc * xc, -1, keepdims=True)
    return xc * lax.rsqrt(var + LN_EPS) * g + b


def _rms_norm(x, g):
    return x * lax.rsqrt(jnp.mean(x * x, -1, keepdims=True) + RMS_EPS) * g


def _dot(a, b):
    return jnp.dot(a, b, preferred_element_type=F32)


def _dot_nt(a, b):
    return lax.dot_general(a, b, (((1,), (1,)), ((), ())), preferred_element_type=F32)


def _ln_kernel(x_ref, g_ref, b_ref, of_ref, ob_ref):
    y = _layer_norm(x_ref[...], g_ref[...], b_ref[...])
    of_ref[...] = y
    ob_ref[...] = y.astype(BF16)


def _ln_call(x, g, b, tm):
    T, D = x.shape
    row = pl.BlockSpec((tm, D), lambda i: (i, 0))
    vec = pl.BlockSpec((1, D), lambda i: (0, 0))
    return pl.pallas_call(
        _ln_kernel, grid=(T // tm,), in_specs=[row, vec, vec], out_specs=[row, row],
        out_shape=[jax.ShapeDtypeStruct((T, D), F32), jax.ShapeDtypeStruct((T, D), BF16)],
        compiler_params=_params(("parallel",)), name="ln_in")(x, g.reshape(1, D), b.reshape(1, D))


def _mm_kernel(a_ref, w_ref, o_ref):
    o_ref[...] = _dot(a_ref[...], w_ref[...]).astype(o_ref.dtype)


def _mm_const_kernel(a_ref, w_ref, c_ref, o_ref):
    o_ref[...] = (_dot(a_ref[...], w_ref[...]) + c_ref[...]).astype(o_ref.dtype)


def _matmul(a, w, out_dtype, tm, tn, name, const_row=None):
    M, K = a.shape
    N = w.shape[1]
    in_specs = [pl.BlockSpec((tm, K), lambda i, j: (i, 0)), pl.BlockSpec((K, tn), lambda i, j: (0, j))]
    args = [a, w]
    if const_row is not None:
        in_specs.append(pl.BlockSpec((1, tn), lambda i, j: (0, j)))
        args.append(const_row)
    return pl.pallas_call(
        _mm_kernel if const_row is None else _mm_const_kernel, grid=(M // tm, N // tn), in_specs=in_specs,
        out_specs=pl.BlockSpec((tm, tn), lambda i, j: (i, j)),
        out_shape=jax.ShapeDtypeStruct((M, N), out_dtype),
        compiler_params=_params(("parallel", "arbitrary")), name=name)(*args)


def _compress_kernel(x_ref, pos_ref, w1_ref, w2_ref, o_ref):
    flat = (x_ref[...] + pos_ref[...]).astype(BF16)
    hid = jax.nn.gelu(_dot(flat, w1_ref[...]))
    o_ref[...] = _dot(hid.astype(BF16), w2_ref[...]).astype(o_ref.dtype)


def _compress_call(flat, pos_flat, w1, w2p, tm, name):
    R, W = flat.shape
    return pl.pallas_call(
        _compress_kernel, grid=(R // tm,),
        in_specs=[pl.BlockSpec((tm, W), lambda i: (i, 0)), pl.BlockSpec((1, W), lambda i: (0, 0)),
                  pl.BlockSpec(w1.shape, lambda i: (0, 0)), pl.BlockSpec(w2p.shape, lambda i: (0, 0))],
        out_specs=pl.BlockSpec((tm, LANES), lambda i: (i, 0)),
        out_shape=jax.ShapeDtypeStruct((R, LANES), BF16),
        compiler_params=_params(("parallel",)), name=name)(flat, pos_flat, w1, w2p)


def _mla_prep_kernel(cq_ref, ckv_ref, kr_ref, krr_ref, cq128_ref, sq128_ref, ck128_ref, sk128_ref,
                     qg_ref, kvg_ref, wq_ref, wqr_ref, wk_ref, wv_ref, q_out, k_out, v_out, *, scale):
    cqn = _rms_norm(cq_ref[...], qg_ref[...]).astype(BF16)
    q_main = _dot(cqn, wq_ref[...])
    q_rot = _dot(cqn, wqr_ref[...])
    ckvn = _rms_norm(ckv_ref[...], kvg_ref[...]).astype(BF16)
    k_nope = _dot(ckvn, wk_ref[...])
    lane = lax.broadcasted_iota(jnp.int32, (1, MLA_HEADS * LANES), 1)
    sum_lane = ((lane & (LANES - 1)) == AUX_LANE).astype(F32)
    v_out[...] = (_dot(ckvn, wv_ref[...]) + sum_lane).astype(BF16)
    k_rope = kr_ref[...] * ck128_ref[...] + krr_ref[...] * sk128_ref[...]
    cq128 = cq128_ref[...]
    sq128 = sq128_ref[...]
    for h in range(MLA_HEADS):
        sl = slice(h * LANES, (h + 1) * LANES)
        q_out[:, sl] = ((q_main[:, sl] * cq128 + q_rot[:, sl] * sq128) * scale).astype(BF16)
        k_out[:, sl] = (k_nope[:, sl] + k_rope).astype(BF16)


def _mla_prep_call(zf, tabs, qg, kvg, wq, wqr, wk, wv, S, tm):
    T = zf.shape[0]
    npos = S // tm
    HW = MLA_HEADS * LANES

    def col(width, off):
        return pl.BlockSpec((tm, width), lambda i: (i, off // width))

    tab = pl.BlockSpec((tm, LANES), lambda i: (i % npos, 0))

    def full(a):
        return pl.BlockSpec(a.shape, lambda i: (0, 0))

    out = pl.BlockSpec((tm, HW), lambda i: (i, 0))
    return pl.pallas_call(
        functools.partial(_mla_prep_kernel, scale=LOG2E * (MLA_NOPE + MLA_ROPE) ** -0.5), grid=(T // tm,),
        in_specs=[col(Q_LORA, ZF_CQ), col(KV_LORA, ZF_CKV), col(LANES, ZF_KR), col(LANES, ZF_KRR),
                  tab, tab, tab, tab, full(qg), full(kvg), full(wq), full(wqr), full(wk), full(wv)],
        out_specs=[out, out, out],
        out_shape=[jax.ShapeDtypeStruct((T, HW), BF16)] * 3,
        compiler_params=_params(("parallel",)), name="mla_prep")(
            zf, zf, zf, zf, *tabs, qg, kvg, wq, wqr, wk, wv)


def _cmp_kernel(q_ref, kc_ref, vc_ref, kctab_ref, mt_ref, o_ref, sel_ref, any_ref, *, tq, nch, nsb, nsel):
    i = pl.program_id(1)
    t0 = i * tq
    r = lax.broadcasted_iota(jnp.int32, (tq, nch), 0)
    c = lax.broadcasted_iota(jnp.int32, (tq, nch), 1)
    valid = (t0 + r >= CMP_STRIDE * c + (CMP_LEN - 1)) & (c < nch - 1)
    row_valid = t0 + lax.broadcasted_iota(jnp.int32, (tq, 1), 0) >= CMP_LEN - 1
    sb = lax.broadcasted_iota(jnp.int32, (LANES, tq), 0)
    sb_f = sb.astype(F32)
    t = t0 + lax.broadcasted_iota(jnp.int32, (LANES, tq), 1)
    cur = lax.shift_right_arithmetic(t, SEL_BLOCK.bit_length() - 1)
    forced = (sb == 0) | (sb == cur) | (sb == cur - 1)
    started = sb * SEL_BLOCK <= t
    mt = mt_ref[...]
    for g in range(NSA_GROUPS):
        kc = kc_ref[0, g] + kctab_ref[...]
        vc = vc_ref[0, g]
        p_grp = jnp.zeros((tq, nch), F32)
        for hh in range(NSA_HPG):
            h = g * NSA_HPG + hh
            sl = slice(h * LANES, (h + 1) * LANES)
            z = jnp.where(valid, _dot_nt(q_ref[:, sl], kc), NEG)
            e = jnp.exp2(z - jnp.max(z, -1, keepdims=True))
            inv = jnp.where(row_valid, 1.0 / jnp.maximum(jnp.sum(e, -1, keepdims=True), 1e-30), 0.0)
            p = e * inv
            o_ref[:, sl] = _dot(p.astype(BF16), vc).astype(BF16)
            p_grp = p_grp + p
        p1 = p_grp.astype(BF16)
        r1 = p_grp - p1.astype(F32)
        p2 = r1.astype(BF16)
        p3 = (r1 - p2.astype(F32)).astype(BF16)
        sc = _dot_nt(mt, p1) + _dot_nt(mt, p2) + _dot_nt(mt, p3)
        sc = jnp.where(forced, sc + FORCE_BONUS, sc)
        sc = jnp.where(started, sc, -FORCE_BONUS)
        sc = jnp.where(sb < nsb, sc, -jnp.inf)
        sel_t = jnp.zeros((LANES, tq), F32)
        for _ in range(nsel):
            m = jnp.max(sc, axis=0, keepdims=True)
            cand = jnp.where(sc == m, sb_f, float(LANES))
            idx = jnp.min(cand, axis=0, keepdims=True)
            hit = sb_f == idx
            sel_t = jnp.where(hit, 1.0, sel_t)
            sc = jnp.where(hit, -jnp.inf, sc)
        sel_mat = sel_t.T
        sel_ref[0, g] = ((sel_mat - 1.0) * MASK_BIG).astype(BF16)
        any_ref[0, 0, g:g + 1, :] = jnp.max(sel_mat, axis=0, keepdims=True)


def _cmp_key_table(nch):
    tab = np.zeros((nch, LANES), np.float32)
    c = np.arange(nch)
    for n in range(ALIBI_TERMS):
        tab[:, AUX_LANE + 2 * n] = CMP_STRIDE * (c // ALIBI_RADIX)
        tab[:, AUX_LANE + 2 * n + 1] = CMP_STRIDE * (c % ALIBI_RADIX)
    return jnp.asarray(tab, BF16)


def _cmp_call(B, S, zb, kcmp, vcmp, mt, tq):
    nch = S // CMP_STRIDE
    nsb = S // SEL_BLOCK
    nq = S // tq
    HW = NSA_HEADS * LANES
    kern = functools.partial(_cmp_kernel, tq=tq, nch=nch, nsb=nsb, nsel=min(N_SELECT, nsb))
    cmp_spec = pl.BlockSpec((1, NSA_GROUPS, nch, LANES), lambda b, i: (b, 0, 0, 0))
    return pl.pallas_call(
        kern, grid=(B, nq),
        in_specs=[pl.BlockSpec((tq, HW), lambda b, i: (b * nq + i, ZB_QN // HW)), cmp_spec, cmp_spec,
                  pl.BlockSpec((nch, LANES), lambda b, i: (0, 0)), pl.BlockSpec(mt.shape, lambda b, i: (0, 0))],
        out_specs=[pl.BlockSpec((tq, HW), lambda b, i: (b * nq + i, 0)),
                   pl.BlockSpec((1, NSA_GROUPS, tq, LANES), lambda b, i: (b, 0, i, 0)),
                   pl.BlockSpec((1, 1, NSA_GROUPS, LANES), lambda b, i: (b, i, 0, 0))],
        out_shape=[jax.ShapeDtypeStruct((B * S, HW), BF16),
                   jax.ShapeDtypeStruct((B, NSA_GROUPS, S, LANES), BF16),
                   jax.ShapeDtypeStruct((B, nq, NSA_GROUPS, LANES), F32)],
        compiler_params=_params(("parallel", "parallel")), name="nsa_cmp")(zb, kcmp, vcmp, _cmp_key_table(nch), mt)


def _flash_kernel(*refs, n_heads, kmap, vmap, slopes, mode, has_sel, sum_lane, finalize, tq, tk, lam_init):
    kj_ref, fl_ref = refs[1], refs[2]
    it = iter(refs[3:])
    q_ref, k_ref, v_ref, rel_ref, ktab_ref = next(it), next(it), next(it), next(it), next(it)
    if has_sel:
        sel_ref, et_ref = next(it), next(it)
    if finalize == "diff":
        lam_ref, subg_ref = next(it), next(it)
    o_ref, m_ref, acc_ref = next(it), next(it), next(it)
    l_ref = None if sum_lane else next(it)

    b = pl.program_id(0)
    s = pl.program_id(1)
    kv = kj_ref[b, s]
    flags = fl_ref[b, s]
    is_diag = (flags & STEP_DIAG) != 0
    is_off = ((flags & STEP_ACTIVE) != 0) & jnp.logical_not(is_diag)

    @pl.when((flags & STEP_FIRST) != 0)
    def _init():
        m_ref[...] = jnp.full(m_ref.shape, NEG, F32)
        acc_ref[...] = jnp.zeros(acc_ref.shape, F32)
        if l_ref is not None:
            l_ref[...] = jnp.zeros(l_ref.shape, F32)

    def run_heads(heads, diag):
        if diag:
            mask = rel_ref[...] <= 0
        elif mode == "window":
            mask = rel_ref[...] > 0
        else:
            mask = None
        key0 = (kv * tk).astype(F32)
        k_blocks = {}

        def scores(h):
            kb = kmap[h]
            if kb not in k_blocks:
                k = k_ref[:, kb * LANES:(kb + 1) * LANES]
                if slopes[h] != 0.0:
                    k = k + ktab_ref[...]
                if has_sel:
                    k = jnp.concatenate([k, et_ref[...]], axis=1)
                k_blocks[kb] = k
            q = q_ref[:, h * LANES:(h + 1) * LANES]
            if has_sel:
                q = jnp.concatenate([q, sel_ref[0, h // NSA_HPG]], axis=1)
            return _dot_nt(q, k_blocks[kb])

        pending = [scores(h) for h in heads[:QK_LOOKAHEAD]]
        for n, h in enumerate(heads):
            u = pending.pop(0)
            if n + QK_LOOKAHEAD < len(heads):
                pending.append(scores(heads[n + QK_LOOKAHEAD]))
            v = v_ref[:, vmap[h] * LANES:(vmap[h] + 1) * LANES]
            delta = slopes[h] * key0 if slopes[h] != 0.0 else 0.0
            if mask is not None:
                u = jnp.where(mask, u, NEG)
            m_prev = m_ref[h]
            m_new = jnp.maximum(m_prev, jnp.max(u, -1, keepdims=True) + delta)
            alpha = jnp.exp2(m_prev - m_new)
            shift = m_new - delta
            psum = None
            chunks = []
            for c in range(tk // LANES):
                pc = jnp.exp2(u[:, c * LANES:(c + 1) * LANES] - shift)
                if l_ref is not None:
                    psum = pc if psum is None else psum + pc
                chunks.append(pc.astype(BF16))
            p = jnp.concatenate(chunks, axis=1)
            if l_ref is not None:
                l_ref[h] = alpha * l_ref[h] + psum
            acc_ref[h] = alpha * acc_ref[h] + _dot(p, v)
            m_ref[h] = m_new

    def normalised(h):
        acc = acc_ref[h]
        if l_ref is None:
            l = acc[:, AUX_LANE:AUX_LANE + 1]
        else:
            l = jnp.sum(l_ref[h], -1, keepdims=True)
        return acc * (1.0 / jnp.maximum(l, 1e-30))

    all_heads = list(range(n_heads))

    @pl.when(is_off)
    def _off():
        if has_sel:
            for g in range(NSA_GROUPS):
                @pl.when((flags & (STEP_GROUP0 << g)) != 0)
                def _group():
                    run_heads(all_heads[g * NSA_HPG:(g + 1) * NSA_HPG], False)
        else:
            run_heads(all_heads, False)

    @pl.when(is_diag)
    def _diag():
        run_heads(all_heads, True)
        if finalize == "plain":
            head_lanes = lax.broadcasted_iota(jnp.int32, (tq, LANES), 1) < HEAD_DIM
            for h in range(n_heads):
                o_ref[:, h * LANES:(h + 1) * LANES] = jnp.where(head_lanes, normalised(h), 0.0).astype(o_ref.dtype)
        else:
            lp = lam_ref[...]
            lam = (jnp.exp(jnp.sum(lp[0:1] * lp[1:2], -1, keepdims=True))
                   - jnp.exp(jnp.sum(lp[2:3] * lp[3:4], -1, keepdims=True)) + lam_init)
            for hd in range(n_heads // 2):
                o = normalised(2 * hd) - lam * normalised(2 * hd + 1)
                o = _rms_norm(o, subg_ref[...]) * (1.0 - lam_init)
                o_ref[:, hd * LANES:(hd + 1) * LANES] = o.astype(o_ref.dtype)


def _static_steps(B, nq, mode):
    qi, kj, fl = [], [], []
    for i in range(nq):
        first_j = 0 if mode == "causal" else max(i - 1, 0)
        for j in range(first_j, i + 1):
            qi.append(i)
            kj.append(j)
            fl.append(STEP_ACTIVE | (STEP_FIRST if j == first_j else 0) | (STEP_DIAG if j == i else 0))
    tile = lambda a: jnp.tile(jnp.asarray(a, jnp.int32)[None], (B, 1))
    return tile(qi), tile(kj), tile(fl)


def _selected_steps(any_sel, S, tq):
    B, nqc, G, _ = any_sel.shape
    nq = nk = S // tq
    per_tile = tq // SEL_BLOCK
    a = any_sel.reshape(B, nq, nqc // nq, G, LANES).max(axis=2) > 0
    a = a[..., :nk * per_tile].reshape(B, nq, G, nk, per_tile).any(-1)
    ii = jnp.arange(nq)[:, None]
    jj = jnp.arange(nk)[None, :]
    g_act = (a & (jj <= ii)[None, :, None, :]) | (ii == jj)[None, :, None, :]
    act = g_act.any(2)
    n_steps = nq * (nq + 1) // 2
    key = jnp.where(act, (ii * nk + jj)[None], nq * nk).reshape(B, nq * nk)
    order = jnp.sort(key, axis=1)[:, :n_steps]
    valid = order < nq * nk
    order = jnp.where(valid, order, nq * nk - 1)
    qi, kj = order // nk, order % nk
    first = valid & (qi != jnp.concatenate([jnp.full((B, 1), -1, qi.dtype), qi[:, :-1]], axis=1))
    g_bits = jnp.take_along_axis(g_act.transpose(0, 2, 1, 3).reshape(B, G, nq * nk), order[:, None, :], axis=2)
    flags = valid * STEP_ACTIVE + first * STEP_FIRST + (valid & (qi == kj)) * STEP_DIAG
    for g in range(G):
        flags = flags + (valid & g_bits[:, g]) * (STEP_GROUP0 << g)
    return qi.astype(jnp.int32), kj.astype(jnp.int32), flags.astype(jnp.int32)


def _flash_call(B, S, q_arr, q_off, n_heads, k_arr, k_off, k_blocks, v_arr, v_off, v_blocks, shared, *, kmap, vmap,
                slopes, mode, tq, name, steps=None, sel=None, lam=None, subg=None, lam_init=0.0):
    tk = tq
    nq = S // tq
    QW, KW, VW = n_heads * LANES, k_blocks * LANES, v_blocks * LANES
    if mode == "window":
        assert tq == WINDOW
    if steps is None:
        steps = _static_steps(B, nq, mode)
    n_steps = steps[0].shape[1]
    finalize = "diff" if lam is not None else "plain"
    sum_lane = finalize == "plain"
    out_heads = n_heads // 2 if finalize == "diff" else n_heads
    in_specs = [pl.BlockSpec((tq, QW), lambda b, s, qi, kj, fl: (b * nq + qi[b, s], q_off // QW)),
                pl.BlockSpec((tk, KW), lambda b, s, qi, kj, fl: (b * nq + kj[b, s], k_off // KW)),
                pl.BlockSpec((tk, VW), lambda b, s, qi, kj, fl: (b * nq + kj[b, s], v_off // VW)),
                pl.BlockSpec((tq, tk), lambda b, s, qi, kj, fl: (0, 0)),
                pl.BlockSpec((tk, LANES), lambda b, s, qi, kj, fl: (0, 0))]
    args = [q_arr, k_arr, v_arr, shared["rel"], shared["ktab"]]
    if sel is not None:
        in_specs += [pl.BlockSpec((1, NSA_GROUPS, tq, LANES), lambda b, s, qi, kj, fl: (b, 0, qi[b, s], 0)),
                     pl.BlockSpec((tk, LANES), lambda b, s, qi, kj, fl: (kj[b, s], 0))]
        args += [sel, shared["block_onehot"]]
    if finalize == "diff":
        in_specs += [pl.BlockSpec(lam.shape, lambda b, s, qi, kj, fl: (0, 0)),
                     pl.BlockSpec(subg.shape, lambda b, s, qi, kj, fl: (0, 0))]
        args += [lam, subg]
    kern = functools.partial(_flash_kernel, n_heads=n_heads, kmap=kmap, vmap=vmap, slopes=slopes, mode=mode,
                             has_sel=sel is not None, sum_lane=sum_lane, finalize=finalize, tq=tq, tk=tk,
                             lam_init=lam_init)
    stat = pltpu.VMEM((n_heads, tq, LANES), F32)
    grid_spec = pltpu.PrefetchScalarGridSpec(
        num_scalar_prefetch=3, grid=(B, n_steps), in_specs=in_specs,
        out_specs=pl.BlockSpec((tq, out_heads * LANES), lambda b, s, qi, kj, fl: (b * nq + qi[b, s], 0)),
        scratch_shapes=[stat, stat] if sum_lane else [stat, stat, stat])
    return pl.pallas_call(
        kern, grid_spec=grid_spec, out_shape=jax.ShapeDtypeStruct((B * S, out_heads * LANES), BF16),
        compiler_params=_params(("parallel", "arbitrary")), name=name)(*steps, *args)


def _combine_kernel(oc_ref, os_ref, ow_ref, ng_ref, od_ref, om_ref, mg0_ref, mg1_ref, mg2_ref, h_ref,
                    eg_ref, wn_ref, wd_ref, wm_ref, wo_ref, g_ref, b_ref, of_ref, ob_ref, *, alpha):
    sg = jax.nn.sigmoid(ng_ref[...]).astype(BF16)
    o_nsa = (_dot(sg, eg_ref[0]) * oc_ref[...].astype(F32)
             + _dot(sg, eg_ref[1]) * os_ref[...].astype(F32)
             + _dot(sg, eg_ref[2]) * ow_ref[...].astype(F32))
    y = (jax.nn.sigmoid(mg0_ref[...]) * _dot(o_nsa.astype(BF16), wn_ref[...])
         + jax.nn.sigmoid(mg1_ref[...]) * _dot(od_ref[...], wd_ref[...])
         + jax.nn.sigmoid(mg2_ref[...]) * _dot(om_ref[...], wm_ref[...]))
    mix = _dot(y.astype(BF16), wo_ref[...])
    hn = _layer_norm(alpha * h_ref[...] + mix, g_ref[...], b_ref[...])
    of_ref[...] = hn
    ob_ref[...] = hn.astype(BF16)


def _combine_call(oc, os_, ow, zf, od, om, h, eg, wn, wd, wm, wo, g, b, alpha, tm):
    T, D = h.shape

    def row(width, blk=0):
        return pl.BlockSpec((tm, width), lambda i: (i, blk))

    def full(a):
        nd = a.ndim
        return pl.BlockSpec(a.shape, lambda i: (0,) * nd)

    return pl.pallas_call(
        functools.partial(_combine_kernel, alpha=alpha), grid=(T // tm,),
        in_specs=[row(oc.shape[1]), row(os_.shape[1]), row(ow.shape[1]), row(LANES, ZF_NG // LANES),
                  row(od.shape[1]), row(om.shape[1]),
                  row(D, ZF_MG // D), row(D, ZF_MG // D + 1), row(D, ZF_MG // D + 2), row(D),
                  full(eg), full(wn), full(wd), full(wm), full(wo), full(g), full(b)],
        out_specs=[row(D), row(D)],
        out_shape=[jax.ShapeDtypeStruct((T, D), F32), jax.ShapeDtypeStruct((T, D), BF16)],
        compiler_params=_params(("parallel",)), name="mixer_combine")(
            oc, os_, ow, zf, od, om, zf, zf, zf, h, eg, wn, wd, wm, wo, g, b)


def _route(logits_t, rb):
    aff = jax.nn.sigmoid(logits_t)
    selv = aff + rb
    a_rows = [aff[e:e + 1] for e in range(N_EXPERTS)]
    s_rows = [selv[e:e + 1] for e in range(N_EXPERTS)]
    npg = EXPERTS_PER_GROUP
    best, grp = None, None
    for g in range(N_GROUPS):
        v = s_rows[g * npg:(g + 1) * npg]
        top2 = None
        for a in range(npg):
            for b in range(a + 1, npg):
                pair = v[a] + v[b]
                top2 = pair if top2 is None else jnp.maximum(top2, pair)
        if g == 0:
            best, grp = top2, jnp.zeros_like(top2, dtype=jnp.int32)
        else:
            better = top2 > best
            grp = jnp.where(better, g, grp)
            best = jnp.where(better, top2, best)

    def pick(rows, k):
        out = rows[k]
        for g in range(1, N_GROUPS):
            out = jnp.where(grp == g, rows[g * npg + k], out)
        return out

    v = [pick(s_rows, k) for k in range(npg)]
    a = [pick(a_rows, k) for k in range(npg)]
    b1, i1 = v[0], jnp.zeros_like(grp)
    for k in range(1, npg):
        gt = v[k] > b1
        i1 = jnp.where(gt, k, i1)
        b1 = jnp.where(gt, v[k], b1)
    b2, i2 = jnp.full_like(b1, -jnp.inf), jnp.zeros_like(grp)
    for k in range(npg):
        ok = (i1 != k) & (v[k] > b2)
        i2 = jnp.where(ok, k, i2)
        b2 = jnp.where(ok, v[k], b2)
    g1 = sum(jnp.where(i1 == k, a[k], 0.0) for k in range(npg))
    g2 = sum(jnp.where(i2 == k, a[k], 0.0) for k in range(npg))
    den = g1 + g2
    w1, w2 = g1 / den, g2 / den
    sub = lax.broadcasted_iota(jnp.int32, (LANES, logits_t.shape[1]), 0)
    comb = jnp.zeros(sub.shape, F32)
    for e in range(N_EXPERTS):
        g, k = divmod(e, npg)
        in_g = grp == g
        row = jnp.where(in_g & (i1 == k), w1, 0.0) + jnp.where(in_g & (i2 == k), w2, 0.0)
        comb = jnp.where(sub == e, row, comb)
    return comb, grp


GRP_LANE, POS_LANE = N_EXPERTS, N_EXPERTS + 1
MOE_CHUNK = 128


def _moe_kernel(x_ref, xb_ref, rwt_ref, rb_ref, tri_ref, w1_ref, w3_ref, w2_ref, g_ref, b_ref, of_ref, ob_ref,
                tok_ref, tok3_ref, rowv_ref, acc_ref, cnt_ref, *, alpha, tm):
    g = pl.program_id(1)

    @pl.when(g == 0)
    def _routing():
        comb_t, grp = _route(_dot_nt(rwt_ref[...], xb_ref[...]), rb_ref[...])
        sub16 = lax.broadcasted_iota(jnp.int32, (16, tm), 0)
        onehot = sub16 == grp
        ranks = _dot(onehot.astype(BF16), tri_ref[...])
        pos = jnp.sum(jnp.where(onehot, ranks, 0.0), axis=0, keepdims=True)
        grp_f = grp.astype(F32)
        for gg in range(N_GROUPS):
            cnt_ref[gg] = jnp.sum((grp == gg).astype(jnp.int32))
        rowv_ref[0:1, :] = grp_f
        rowv_ref[1:2, :] = pos
        sub = lax.broadcasted_iota(jnp.int32, (LANES, tm), 0)
        tok = jnp.where(sub == GRP_LANE, grp_f, jnp.where(sub == POS_LANE, pos, comb_t)).T
        tok_ref[...] = tok
        t1 = tok.astype(BF16)
        r1 = tok - t1.astype(F32)
        t2 = r1.astype(BF16)
        tok3_ref[0] = t1
        tok3_ref[1] = t2
        tok3_ref[2] = (r1 - t2.astype(F32)).astype(BF16)
        acc_ref[...] = jnp.zeros(acc_ref.shape, F32)

    gf = g.astype(F32)
    in_g_row = rowv_ref[0:1, :] == gf
    pos_row = rowv_ref[1:2, :]
    tok = tok_ref[...]
    in_g_col = tok[:, GRP_LANE:GRP_LANE + 1] == gf
    pos_col = tok[:, POS_LANE:POS_LANE + 1]
    r_sub = lax.broadcasted_iota(jnp.int32, (MOE_CHUNK, tm), 0).astype(F32)
    r_lane = lax.broadcasted_iota(jnp.int32, (tm, MOE_CHUNK), 1).astype(F32)
    lane = lax.broadcasted_iota(jnp.int32, (MOE_CHUNK, LANES), 1)

    def chunk(c, carry):
        base = (c * MOE_CHUNK).astype(F32)
        gather = (in_g_row & (pos_row - base == r_sub)).astype(BF16)
        scatter = (in_g_col & (pos_col - base == r_lane)).astype(BF16)
        xg = _dot(gather, xb_ref[...]).astype(BF16)
        cg = _dot(gather, tok3_ref[0]) + _dot(gather, tok3_ref[1]) + _dot(gather, tok3_ref[2])
        y = jnp.zeros((MOE_CHUNK, x_ref.shape[1]), F32)
        for k in range(EXPERTS_PER_GROUP):
            col = jnp.sum(jnp.where(lane == g * EXPERTS_PER_GROUP + k, cg, 0.0), -1, keepdims=True)
            hid = jax.nn.silu(_dot(xg, w1_ref[k])) * _dot(xg, w3_ref[k]) * col
            y = y + _dot(hid.astype(BF16), w2_ref[k])
        acc_ref[...] += _dot(scatter, y.astype(BF16))
        return carry

    lax.fori_loop(0, (cnt_ref[g] + MOE_CHUNK - 1) // MOE_CHUNK, chunk, 0)

    @pl.when(g == N_GROUPS - 1)
    def _finish():
        hn = _layer_norm(alpha * x_ref[...] + acc_ref[...], g_ref[...], b_ref[...])
        of_ref[...] = hn
        ob_ref[...] = hn.astype(BF16)


def _moe_call(h, hb, rwt, rb, w1, w3, w2, g, b, alpha, tm):
    T, D = h.shape
    F = w1.shape[2]
    row = pl.BlockSpec((tm, D), lambda i, e: (i, 0))
    idx = jnp.arange(tm)
    tri = (idx[:, None] < idx[None, :]).astype(BF16)

    def full(a):
        return pl.BlockSpec(a.shape, lambda i, e: (0, 0))

    return pl.pallas_call(
        functools.partial(_moe_kernel, alpha=alpha, tm=tm), grid=(T // tm, N_GROUPS),
        in_specs=[row, row, full(rwt), full(rb), full(tri),
                  pl.BlockSpec((EXPERTS_PER_GROUP, D, F), lambda i, e: (e, 0, 0)),
                  pl.BlockSpec((EXPERTS_PER_GROUP, D, F), lambda i, e: (e, 0, 0)),
                  pl.BlockSpec((EXPERTS_PER_GROUP, F, D), lambda i, e: (e, 0, 0)), full(g), full(b)],
        out_specs=[row, row],
        out_shape=[jax.ShapeDtypeStruct((T, D), F32), jax.ShapeDtypeStruct((T, D), BF16)],
        scratch_shapes=[pltpu.VMEM((tm, LANES), F32), pltpu.VMEM((3, tm, LANES), BF16), pltpu.VMEM((8, tm), F32),
                        pltpu.VMEM((tm, D), F32), pltpu.SMEM((N_GROUPS,), jnp.int32)],
        compiler_params=_params(("parallel", "arbitrary")), name="moe")(h, hb, rwt, rb, tri, w1, w3, w2, g, b)


def _head_pad_cols(w, n_heads, width, scale=1.0):
    K = w.shape[0]
    w = (w * scale).reshape(K, n_heads, width)
    return jnp.pad(w, ((0, 0), (0, 0), (0, LANES - width))).reshape(K, n_heads * LANES)


def _head_pad_rows(w, n_heads, width):
    N = w.shape[1]
    w = w.reshape(n_heads, width, N)
    return jnp.pad(w, ((0, 0), (0, LANES - width), (0, 0))).reshape(n_heads * LANES, N)


def _rot_half_cols(w):
    half = w.shape[1] // 2
    return jnp.concatenate([-w[:, half:], w[:, :half]], axis=1)


def _in_proj_weights(w_in):
    D = w_in.shape[0]
    widths = (NSA_HEADS * HEAD_DIM,) + (NSA_GROUPS * HEAD_DIM,) * 6 + (
        3 * NSA_HEADS, DIFF_HEADS * 2 * DIFF_DIM, DIFF_HEADS * 2 * DIFF_DIM, DIFF_HEADS * 2 * DIFF_DIM,
        Q_LORA, KV_LORA, MLA_ROPE, 3 * D)
    parts, o = [], 0
    for w in widths:
        parts.append(w_in[:, o:o + w])
        o += w
    nq, kc, vc, ks, vs, kw, vw, ng, dq, dk, dv, cq, ckv, kr, mg = parts
    wb = jnp.concatenate([
        _head_pad_cols(nq, NSA_HEADS, HEAD_DIM, LOG2E * HEAD_DIM ** -0.5),
        _head_pad_cols(dq, 2 * DIFF_HEADS, DIFF_DIM, LOG2E * DIFF_DIM ** -0.5),
        _head_pad_cols(dk, 2 * DIFF_HEADS, DIFF_DIM),
        _head_pad_cols(ks, NSA_GROUPS, HEAD_DIM), _head_pad_cols(vs, NSA_GROUPS, HEAD_DIM),
        _head_pad_cols(kw, NSA_GROUPS, HEAD_DIM), _head_pad_cols(vw, NSA_GROUPS, HEAD_DIM),
        dv], axis=1).astype(BF16)

    def rope_block(w):
        return jnp.pad(w, ((0, 0), (MLA_NOPE, LANES - MLA_NOPE - MLA_ROPE)))

    wf = jnp.concatenate([
        cq, kc, vc, jnp.pad(ng, ((0, 0), (0, LANES - ng.shape[1]))), ckv,
        rope_block(kr), rope_block(_rot_half_cols(kr)), mg], axis=1).astype(BF16)
    assert wb.shape[1] == ZB_WIDTH and wf.shape[1] == ZF_WIDTH
    return wb, wf


def _bf16_terms(x, n):
    terms, rest = [], np.float32(x)
    for _ in range(n):
        t = np.float32(np.asarray(rest, dtype=BF16).astype(np.float32))
        terms.append(float(t))
        rest = np.float32(rest - t)
    return terms


def _aux_const_row(slopes):
    row = np.zeros((1, ZB_WIDTH), np.float32)
    q_blocks = [(ZB_QN + h * LANES, slopes[h]) for h in range(NSA_HEADS)]
    q_blocks += [(ZB_DQ + b * LANES, slopes[NSA_HEADS + b // 2]) for b in range(2 * DIFF_HEADS)]
    for off, slope in q_blocks:
        for n, term in enumerate(_bf16_terms(slope, ALIBI_TERMS)):
            row[0, off + AUX_LANE + 2 * n] = ALIBI_RADIX * term
            row[0, off + AUX_LANE + 2 * n + 1] = term
    for off in (ZB_VS, ZB_VW):
        for g in range(NSA_GROUPS):
            row[0, off + g * LANES + AUX_LANE] = 1.0
    return jnp.asarray(row)


def _key_offset_table(tk):
    tab = np.zeros((tk, LANES), np.float32)
    c = np.arange(tk)
    for n in range(ALIBI_TERMS):
        tab[:, AUX_LANE + 2 * n] = c // ALIBI_RADIX
        tab[:, AUX_LANE + 2 * n + 1] = c % ALIBI_RADIX
    return jnp.asarray(tab, BF16)


def _rope_tables(S):
    half = MLA_ROPE // 2
    freqs = ROPE_THETA ** (-jnp.arange(half, dtype=F32) / half)
    ang = jnp.arange(S, dtype=F32)[:, None] * freqs[None, :]
    cos = jnp.concatenate([jnp.cos(ang), jnp.cos(ang)], -1)
    sin = jnp.concatenate([jnp.sin(ang), jnp.sin(ang)], -1)
    tail = jnp.zeros((S, LANES - MLA_NOPE - MLA_ROPE), F32)
    cos_q = jnp.concatenate([jnp.ones((S, MLA_NOPE), F32), cos, tail], -1)
    cos_k = jnp.concatenate([jnp.zeros((S, MLA_NOPE), F32), cos, tail], -1)
    sin_qk = jnp.concatenate([jnp.zeros((S, MLA_NOPE), F32), sin, tail], -1)
    return cos_q, sin_qk, cos_k, sin_qk


def _score_matrix_t(S):
    nch = S // CMP_STRIDE
    ratio = CMP_LEN // CMP_STRIDE
    per_sb = SEL_BLOCK // CMP_STRIDE
    sb = jnp.arange(LANES)[:, None]
    cb = jnp.arange(nch)[None, :]
    m = jnp.zeros((LANES, nch), F32)
    for jj in range(ratio):
        chunk = cb + jj
        m = m + ((chunk // per_sb == sb) & (chunk < nch)).astype(F32)
    return m.astype(BF16)


def _cmp_flat(z, B, S):
    nch = S // CMP_STRIDE
    x = z.reshape(B, S, NSA_GROUPS, HEAD_DIM).transpose(0, 2, 1, 3).reshape(B, NSA_GROUPS, nch, CMP_STRIDE * HEAD_DIM)
    nxt = jnp.roll(x, -1, axis=2)
    return jnp.concatenate([x, nxt], -1).reshape(B * NSA_GROUPS * nch, CMP_LEN * HEAD_DIM)


def _mixer_and_ffn(h, hb, layer, B, S, p, shared, alpha):
    T, D = h.shape
    tq = min(WINDOW, S)
    wb, wf = _in_proj_weights(p["w_in"])
    zb = _matmul(hb, wb, BF16, min(1024, T), ZB_WIDTH // 3, "in_proj_b", const_row=shared["aux_row"])
    zf = _matmul(hb, wf, F32, min(1024, T), ZF_WIDTH // 4, "in_proj_f")
    slopes = _alibi_slopes_log2()
    nsa_slopes, diff_slopes = slopes[:NSA_HEADS], slopes[NSA_HEADS:]

    nch = S // CMP_STRIDE
    cmp_out = []
    for off, pos, w1, w2 in ((ZF_KC, p["cmp_pos_k"], p["cmp_w1_k"], p["cmp_w2_k"]),
                             (ZF_VC, p["cmp_pos_v"], p["cmp_w1_v"], p["cmp_w2_v"])):
        flat = _cmp_flat(zf[:, off:off + NSA_GROUPS * HEAD_DIM], B, S)
        w2p = jnp.pad(w2, ((0, 0), (0, LANES - HEAD_DIM))).astype(BF16)
        out = _compress_call(flat, pos.reshape(1, CMP_LEN * HEAD_DIM), w1.astype(BF16), w2p,
                             min(512, flat.shape[0]), "nsa_compress")
        cmp_out.append(out.reshape(B, NSA_GROUPS, nch, LANES))
    o_cmp, sel, any_sel = _cmp_call(B, S, zb, cmp_out[0], cmp_out[1], shared["score_t"], min(256, S))
    grp_map = [h_ // NSA_HPG for h_ in range(NSA_HEADS)]
    o_sel = _flash_call(B, S, zb, ZB_QN, NSA_HEADS, zb, ZB_KS, NSA_GROUPS, zb, ZB_VS, NSA_GROUPS, shared,
                        kmap=grp_map, vmap=grp_map, slopes=nsa_slopes, mode="causal", tq=tq, name="nsa_sel", sel=sel,
                        steps=_selected_steps(any_sel, S, tq))
    o_win = _flash_call(B, S, zb, ZB_QN, NSA_HEADS, zb, ZB_KW, NSA_GROUPS, zb, ZB_VW, NSA_GROUPS, shared,
                        kmap=grp_map, vmap=grp_map, slopes=nsa_slopes, mode="window", tq=tq, name="nsa_win")

    lam_init = 0.8 - 0.6 * math.exp(-0.3 * layer)
    n_maps = 2 * DIFF_HEADS
    o_diff = _flash_call(B, S, zb, ZB_DQ, n_maps, zb, ZB_DK, n_maps, zb, ZB_DV, DIFF_HEADS, shared,
                         kmap=list(range(n_maps)), vmap=[m_ // 2 for m_ in range(n_maps)],
                         slopes=[diff_slopes[m_ // 2] for m_ in range(n_maps)],
                         mode="causal", tq=tq, name="diff_attn", lam=p["diff_lambda"],
                         subg=p["diff_subln_g"].reshape(1, 2 * DIFF_DIM), lam_init=lam_init)

    w_uq = p["mla_w_uq"].reshape(Q_LORA, MLA_HEADS, MLA_NOPE + MLA_ROPE)
    wq = jnp.pad(w_uq, ((0, 0), (0, 0), (0, LANES - MLA_NOPE - MLA_ROPE))).reshape(Q_LORA, MLA_HEADS * LANES)
    rot = jnp.stack([_rot_half_cols(w_uq[:, h_, MLA_NOPE:]) for h_ in range(MLA_HEADS)], axis=1)
    wqr = jnp.pad(rot, ((0, 0), (0, 0), (MLA_NOPE, LANES - MLA_NOPE - MLA_ROPE))).reshape(Q_LORA, MLA_HEADS * LANES)
    w_ukv = p["mla_w_ukv"].reshape(KV_LORA, MLA_HEADS, MLA_NOPE + MLA_V)
    wk = _head_pad_cols(w_ukv[:, :, :MLA_NOPE].reshape(KV_LORA, -1), MLA_HEADS, MLA_NOPE)
    wv = _head_pad_cols(w_ukv[:, :, MLA_NOPE:].reshape(KV_LORA, -1), MLA_HEADS, MLA_V)
    qm, km, vm = _mla_prep_call(zf, shared["rope"], p["mla_q_norm_g"].reshape(1, Q_LORA),
                                p["mla_kv_norm_g"].reshape(1, KV_LORA), wq.astype(BF16), wqr.astype(BF16),
                                wk.astype(BF16), wv.astype(BF16), S, min(512, S))
    ident = list(range(MLA_HEADS))
    o_mla = _flash_call(B, S, qm, 0, MLA_HEADS, km, 0, MLA_HEADS, vm, 0, MLA_HEADS, shared, kmap=ident, vmap=ident,
                        slopes=[0.0] * MLA_HEADS, mode="causal", tq=tq, name="mla_attn")

    h1, h1b = _combine_call(
        o_cmp, o_sel, o_win, zf, o_diff, o_mla, h, shared["gate_expand"],
        _head_pad_rows(p["w_br_nsa"], NSA_HEADS, HEAD_DIM).astype(BF16), p["w_br_diff"].astype(BF16),
        _head_pad_rows(p["w_br_mla"], MLA_HEADS, MLA_V).astype(BF16), p["w_out"].astype(BF16),
        p["ln1_g"].reshape(1, D), p["ln1_b"].reshape(1, D), alpha, min(256, T))

    return _moe_call(h1, h1b, shared["router_wt"], shared["router_b"], p["moe_w1"].astype(BF16),
                     p["moe_w3"].astype(BF16), p["moe_w2"].astype(BF16), p["ln2_g"].reshape(1, D),
                     p["ln2_b"].reshape(1, D), alpha, min(512, T))


def kernel(x, ln_in_g, ln_in_b, w_in, cmp_pos_k, cmp_w1_k, cmp_w2_k, cmp_pos_v, cmp_w1_v, cmp_w2_v, diff_lambda, diff_subln_g, mla_q_norm_g, mla_kv_norm_g, mla_w_uq, mla_w_ukv, w_br_nsa, w_br_diff, w_br_mla, w_out, ln1_g, ln1_b, router_w, router_b, moe_w1, moe_w3, moe_w2, ln2_g, ln2_b):
    B, S, D = x.shape
    depth = w_in.shape[0]
    alpha = (2 * depth) ** 0.25
    T = B * S
    tq = min(WINDOW, S)
    idx = jnp.arange(tq, dtype=jnp.int32)
    gate_rows = jnp.arange(LANES)[:, None]
    gate_cols = jnp.arange(NSA_HEADS * LANES)[None, :] // LANES
    shared = {
        "rel": (idx[None, :] - idx[:, None]).astype(F32),
        "ktab": _key_offset_table(tq),
        "aux_row": _aux_const_row(_alibi_slopes_log2()),
        "block_onehot": ((jnp.arange(S)[:, None] // SEL_BLOCK) == jnp.arange(LANES)[None, :]).astype(BF16),
        "score_t": _score_matrix_t(S),
        "rope": _rope_tables(S),
        "gate_expand": jnp.stack([(gate_rows == gate_cols * 3 + j) for j in range(3)]).astype(BF16),
        "router_wt": router_w.T.astype(BF16),
        "router_b": router_b.reshape(N_EXPERTS, 1).astype(F32),
    }
    per_layer = dict(w_in=w_in, cmp_pos_k=cmp_pos_k, cmp_w1_k=cmp_w1_k, cmp_w2_k=cmp_w2_k, cmp_pos_v=cmp_pos_v,
                     cmp_w1_v=cmp_w1_v, cmp_w2_v=cmp_w2_v, diff_lambda=diff_lambda, diff_subln_g=diff_subln_g,
                     mla_q_norm_g=mla_q_norm_g, mla_kv_norm_g=mla_kv_norm_g, mla_w_uq=mla_w_uq, mla_w_ukv=mla_w_ukv,
                     w_br_nsa=w_br_nsa, w_br_diff=w_br_diff, w_br_mla=w_br_mla, w_out=w_out, ln1_g=ln1_g,
                     ln1_b=ln1_b, moe_w1=moe_w1, moe_w3=moe_w3, moe_w2=moe_w2, ln2_g=ln2_g, ln2_b=ln2_b)
    h, hb = _ln_call(x.reshape(T, D), ln_in_g, ln_in_b, min(512, T))
    for l in range(depth):
        p = {k: v[l] for k, v in per_layer.items()}
        h, hb = _mixer_and_ffn(h, hb, l, B, S, p, shared, alpha)
    return h.reshape(B, S, D)
```

```python
import functools
import math

import numpy as np
import jax
import jax.numpy as jnp
from jax import lax
from jax.experimental import pallas as pl
from jax.experimental.pallas import tpu as pltpu

F32 = jnp.float32
BF16 = jnp.bfloat16

LANES = 128
HEAD_DIM = 64
NSA_HEADS = 8
NSA_GROUPS = 2
NSA_HPG = NSA_HEADS // NSA_GROUPS
CMP_LEN = 32
CMP_STRIDE = 16
CMP_HIDDEN = 256
SEL_BLOCK = 64
N_SELECT = 16
WINDOW = 512
FORCE_BONUS = 1.0e4
DIFF_HEADS = 4
DIFF_DIM = 64
MLA_HEADS = 8
MLA_NOPE = 64
MLA_ROPE = 32
MLA_V = 64
Q_LORA = 256
KV_LORA = 128
ROPE_THETA = 10000.0
N_EXPERTS = 16
N_GROUPS = 4
EXPERTS_PER_GROUP = N_EXPERTS // N_GROUPS
D_FF_EXPERT = 512
LN_EPS = 1e-5
RMS_EPS = 1e-6
NEG = -1e30
LOG2E = math.log2(math.e)
MASK_BIG = 2.0 ** 100
AUX_LANE = HEAD_DIM
ALIBI_TERMS = 3
ALIBI_RADIX = 16
QK_LOOKAHEAD = 2
STEP_ACTIVE, STEP_FIRST, STEP_DIAG, STEP_GROUP0, STEP_HIGH = 1, 2, 4, 8, 32

VMEM_LIMIT = 56 * 1024 * 1024

ZB_QN, ZB_DQ, ZB_DK, ZB_KS, ZB_VS, ZB_KW, ZB_VW, ZB_DV, ZB_WIDTH = 0, 1024, 2048, 3072, 3328, 3584, 3840, 4096, 4608
ZF_CQ, ZF_KC, ZF_VC, ZF_NG, ZF_CKV, ZF_KR, ZF_KRR, ZF_MG, ZF_WIDTH = 0, 256, 384, 512, 640, 768, 896, 1024, 4096


def _alibi_slopes_log2():
    n = NSA_HEADS + DIFF_HEADS
    return [LOG2E * 2.0 ** (-8.0 * i / n) for i in range(1, n + 1)]


def _params(sem):
    return pltpu.CompilerParams(dimension_semantics=sem, vmem_limit_bytes=VMEM_LIMIT)


def _layer_norm(x, g, b):
    mu = jnp.mean(x, -1, keepdims=True)
    xc = x - mu
    var = jnp.mean(xc * xc, -1, keepdims=True)
    return xc * lax.rsqrt(var + LN_EPS) * g + b


def _rms_norm(x, g):
    return x * lax.rsqrt(jnp.mean(x * x, -1, keepdims=True) + RMS_EPS) * g


def _dot(a, b):
    return jnp.dot(a, b, preferred_element_type=F32)


def _dot_nt(a, b):
    return lax.dot_general(a, b, (((1,), (1,)), ((), ())), preferred_element_type=F32)


def _ln_kernel(x_ref, g_ref, b_ref, of_ref, ob_ref):
    y = _layer_norm(x_ref[...], g_ref[...], b_ref[...])
    of_ref[...] = y
    ob_ref[...] = y.astype(BF16)


def _ln_call(x, g, b, tm):
    T, D = x.shape
    row = pl.BlockSpec((tm, D), lambda i: (i, 0))
    vec = pl.BlockSpec((1, D), lambda i: (0, 0))
    return pl.pallas_call(
        _ln_kernel, grid=(T // tm,), in_specs=[row, vec, vec], out_specs=[row, row],
        out_shape=[jax.ShapeDtypeStruct((T, D), F32), jax.ShapeDtypeStruct((T, D), BF16)],
        compiler_params=_params(("parallel",)), name="ln_in")(x, g.reshape(1, D), b.reshape(1, D))


def _mm_kernel(a_ref, w_ref, o_ref):
    o_ref[...] = _dot(a_ref[...], w_ref[...]).astype(o_ref.dtype)


def _mm_const_kernel(a_ref, w_ref, c_ref, o_ref):
    o_ref[...] = (_dot(a_ref[...], w_ref[...]) + c_ref[...]).astype(o_ref.dtype)


def _matmul(a, w, out_dtype, tm, tn, name, const_row=None):
    M, K = a.shape
    N = w.shape[1]
    in_specs = [pl.BlockSpec((tm, K), lambda i, j: (i, 0)), pl.BlockSpec((K, tn), lambda i, j: (0, j))]
    args = [a, w]
    if const_row is not None:
        in_specs.append(pl.BlockSpec((1, tn), lambda i, j: (0, j)))
        args.append(const_row)
    return pl.pallas_call(
        _mm_kernel if const_row is None else _mm_const_kernel, grid=(M // tm, N // tn), in_specs=in_specs,
        out_specs=pl.BlockSpec((tm, tn), lambda i, j: (i, j)),
        out_shape=jax.ShapeDtypeStruct((M, N), out_dtype),
        compiler_params=_params(("parallel", "arbitrary")), name=name)(*args)


def _compress_kernel(x_ref, pos_ref, w1_ref, w2_ref, o_ref):
    flat = (x_ref[...] + pos_ref[...]).astype(BF16)
    hid = jax.nn.gelu(_dot(flat, w1_ref[...]))
    o_ref[...] = _dot(hid.astype(BF16), w2_ref[...]).astype(o_ref.dtype)


def _compress_call(flat, pos_flat, w1, w2p, tm, name):
    R, W = flat.shape
    return pl.pallas_call(
        _compress_kernel, grid=(R // tm,),
        in_specs=[pl.BlockSpec((tm, W), lambda i: (i, 0)), pl.BlockSpec((1, W), lambda i: (0, 0)),
                  pl.BlockSpec(w1.shape, lambda i: (0, 0)), pl.BlockSpec(w2p.shape, lambda i: (0, 0))],
        out_specs=pl.BlockSpec((tm, LANES), lambda i: (i, 0)),
        out_shape=jax.ShapeDtypeStruct((R, LANES), BF16),
        compiler_params=_params(("parallel",)), name=name)(flat, pos_flat, w1, w2p)


def _mla_prep_kernel(cq_ref, ckv_ref, kr_ref, krr_ref, cq128_ref, sq128_ref, ck128_ref, sk128_ref,
                     qg_ref, kvg_ref, wq_ref, wqr_ref, wk_ref, wv_ref, q_out, k_out, v_out, *, scale):
    cqn = _rms_norm(cq_ref[...], qg_ref[...]).astype(BF16)
    q_main = _dot(cqn, wq_ref[...])
    q_rot = _dot(cqn, wqr_ref[...])
    ckvn = _rms_norm(ckv_ref[...], kvg_ref[...]).astype(BF16)
    k_nope = _dot(ckvn, wk_ref[...])
    lane = lax.broadcasted_iota(jnp.int32, (1, MLA_HEADS * LANES), 1)
    sum_lane = ((lane & (LANES - 1)) == AUX_LANE).astype(F32)
    v_out[...] = (_dot(ckvn, wv_ref[...]) + sum_lane).astype(BF16)
    k_rope = kr_ref[...] * ck128_ref[...] + krr_ref[...] * sk128_ref[...]
    cq128 = cq128_ref[...]
    sq128 = sq128_ref[...]
    for h in range(MLA_HEADS):
        sl = slice(h * LANES, (h + 1) * LANES)
        q_out[:, sl] = ((q_main[:, sl] * cq128 + q_rot[:, sl] * sq128) * scale).astype(BF16)
        k_out[:, sl] = (k_nope[:, sl] + k_rope).astype(BF16)


def _mla_prep_call(zf, tabs, qg, kvg, wq, wqr, wk, wv, S, tm):
    T = zf.shape[0]
    npos = S // tm
    HW = MLA_HEADS * LANES

    def col(width, off):
        return pl.BlockSpec((tm, width), lambda i: (i, off // width))

    tab = pl.BlockSpec((tm, LANES), lambda i: (i % npos, 0))

    def full(a):
        return pl.BlockSpec(a.shape, lambda i: (0, 0))

    out = pl.BlockSpec((tm, HW), lambda i: (i, 0))
    return pl.pallas_call(
        functools.partial(_mla_prep_kernel, scale=LOG2E * (MLA_NOPE + MLA_ROPE) ** -0.5), grid=(T // tm,),
        in_specs=[col(Q_LORA, ZF_CQ), col(KV_LORA, ZF_CKV), col(LANES, ZF_KR), col(LANES, ZF_KRR),
                  tab, tab, tab, tab, full(qg), full(kvg), full(wq), full(wqr), full(wk), full(wv)],
        out_specs=[out, out, out],
        out_shape=[jax.ShapeDtypeStruct((T, HW), BF16)] * 3,
        compiler_params=_params(("parallel",)), name="mla_prep")(
            zf, zf, zf, zf, *tabs, qg, kvg, wq, wqr, wk, wv)


def _cmp_kernel(q_ref, kc_ref, vc_ref, kctab_ref, mt_ref, o_ref, sel_ref, any_ref, *, tq, nch, nsb, nsel):
    i = pl.program_id(1)
    t0 = i * tq
    r = lax.broadcasted_iota(jnp.int32, (tq, nch), 0)
    c = lax.broadcasted_iota(jnp.int32, (tq, nch), 1)
    valid = (t0 + r >= CMP_STRIDE * c + (CMP_LEN - 1)) & (c < nch - 1)
    row_valid = t0 + lax.broadcasted_iota(jnp.int32, (tq, 1), 0) >= CMP_LEN - 1
    sb = lax.broadcasted_iota(jnp.int32, (LANES, tq), 0)
    sb_f = sb.astype(F32)
    t = t0 + lax.broadcasted_iota(jnp.int32, (LANES, tq), 1)
    cur = lax.shift_right_arithmetic(t, SEL_BLOCK.bit_length() - 1)
    forced = (sb == 0) | (sb == cur) | (sb == cur - 1)
    started = sb * SEL_BLOCK <= t
    mt = mt_ref[...]
    for g in range(NSA_GROUPS):
        kc = kc_ref[0, g] + kctab_ref[...]
        vc = vc_ref[0, g]
        p_grp = jnp.zeros((tq, nch), F32)
        for hh in range(NSA_HPG):
            h = g * NSA_HPG + hh
            sl = slice(h * LANES, (h + 1) * LANES)
            z = jnp.where(valid, _dot_nt(q_ref[:, sl], kc), NEG)
            e = jnp.exp2(z - jnp.max(z, -1, keepdims=True))
            inv = jnp.where(row_valid, 1.0 / jnp.maximum(jnp.sum(e, -1, keepdims=True), 1e-30), 0.0)
            p = e * inv
            o_ref[:, sl] = _dot(p.astype(BF16), vc).astype(BF16)
            p_grp = p_grp + p
        p1 = p_grp.astype(BF16)
        r1 = p_grp - p1.astype(F32)
        p2 = r1.astype(BF16)
        p3 = (r1 - p2.astype(F32)).astype(BF16)
        sc = _dot_nt(mt, p1) + _dot_nt(mt, p2) + _dot_nt(mt, p3)
        sc = jnp.where(forced, sc + FORCE_BONUS, sc)
        sc = jnp.where(started, sc, -FORCE_BONUS)
        sc = jnp.where(sb < nsb, sc, -jnp.inf)
        sel_t = jnp.zeros((LANES, tq), F32)
        for _ in range(nsel):
            m = jnp.max(sc, axis=0, keepdims=True)
            cand = jnp.where(sc == m, sb_f, float(LANES))
            idx = jnp.min(cand, axis=0, keepdims=True)
            hit = sb_f == idx
            sel_t = jnp.where(hit, 1.0, sel_t)
            sc = jnp.where(hit, -jnp.inf, sc)
        sel_mat = sel_t.T
        sel_ref[0, g] = ((sel_mat - 1.0) * MASK_BIG).astype(BF16)
        any_ref[0, 0, g:g + 1, :] = jnp.max(sel_mat, axis=0, keepdims=True)


def _cmp_key_table(nch):
    tab = np.zeros((nch, LANES), np.float32)
    c = np.arange(nch)
    for n in range(ALIBI_TERMS):
        tab[:, AUX_LANE + 2 * n] = CMP_STRIDE * (c // ALIBI_RADIX)
        tab[:, AUX_LANE + 2 * n + 1] = CMP_STRIDE * (c % ALIBI_RADIX)
    return jnp.asarray(tab, BF16)


def _cmp_call(B, S, zb, kcmp, vcmp, mt, tq):
    nch = S // CMP_STRIDE
    nsb = S // SEL_BLOCK
    nq = S // tq
    HW = NSA_HEADS * LANES
    kern = functools.partial(_cmp_kernel, tq=tq, nch=nch, nsb=nsb, nsel=min(N_SELECT, nsb))
    cmp_spec = pl.BlockSpec((1, NSA_GROUPS, nch, LANES), lambda b, i: (b, 0, 0, 0))
    return pl.pallas_call(
        kern, grid=(B, nq),
        in_specs=[pl.BlockSpec((tq, HW), lambda b, i: (b * nq + i, ZB_QN // HW)), cmp_spec, cmp_spec,
                  pl.BlockSpec((nch, LANES), lambda b, i: (0, 0)), pl.BlockSpec(mt.shape, lambda b, i: (0, 0))],
        out_specs=[pl.BlockSpec((tq, HW), lambda b, i: (b * nq + i, 0)),
                   pl.BlockSpec((1, NSA_GROUPS, tq, LANES), lambda b, i: (b, 0, i, 0)),
                   pl.BlockSpec((1, 1, NSA_GROUPS, LANES), lambda b, i: (b, i, 0, 0))],
        out_shape=[jax.ShapeDtypeStruct((B * S, HW), BF16),
                   jax.ShapeDtypeStruct((B, NSA_GROUPS, S, LANES), BF16),
                   jax.ShapeDtypeStruct((B, nq, NSA_GROUPS, LANES), F32)],
        compiler_params=_params(("parallel", "parallel")), name="nsa_cmp")(zb, kcmp, vcmp, _cmp_key_table(nch), mt)


def _flash_kernel(*refs, n_heads, kmap, vmap, slopes, mode, has_sel, sum_lane, finalize, tq, tk, lam_init):
    kj_ref, fl_ref = refs[1], refs[2]
    it = iter(refs[3:])
    q_ref, k_ref, v_ref, rel_ref, ktab_ref = next(it), next(it), next(it), next(it), next(it)
    if has_sel:
        sel_ref, et_ref = next(it), next(it)
    if finalize == "diff":
        lam_ref, subg_ref = next(it), next(it)
    o_ref, m_ref, acc_ref = next(it), next(it), next(it)
    l_ref = None if sum_lane else next(it)

    b = pl.program_id(0)
    s = pl.program_id(1)
    kv = kj_ref[b, s]
    flags = fl_ref[b, s]
    is_diag = (flags & STEP_DIAG) != 0
    is_off = ((flags & STEP_ACTIVE) != 0) & jnp.logical_not(is_diag)

    @pl.when((flags & STEP_FIRST) != 0)
    def _init():
        m_ref[...] = jnp.full(m_ref.shape, NEG, F32)
        acc_ref[...] = jnp.zeros(acc_ref.shape, F32)
        if l_ref is not None:
            l_ref[...] = jnp.zeros(l_ref.shape, F32)

    def run_heads(heads, variant):
        rows = tq if variant == "diag_lo" else tk
        if variant in ("diag", "diag_lo"):
            mask = rel_ref[...] <= 0
        elif variant == "diag_hi":
            mask = jnp.concatenate([rel_ref[...] - tq, rel_ref[...]], axis=1) <= 0
        elif variant == "prev":
            mask = rel_ref[...] > 0
        else:
            mask = None
        key0 = (kv * tk).astype(F32)
        k_blocks = {}

        def scores(h):
            kb = kmap[h]
            if kb not in k_blocks:
                k = k_ref[:rows, kb * LANES:(kb + 1) * LANES]
                if slopes[h] != 0.0:
                    k = k + ktab_ref[:rows, :]
                if has_sel:
                    k = jnp.concatenate([k, et_ref[...]], axis=1)
                k_blocks[kb] = k
            q = q_ref[:, h * LANES:(h + 1) * LANES]
            if has_sel:
                q = jnp.concatenate([q, sel_ref[0, h // NSA_HPG]], axis=1)
            return _dot_nt(q, k_blocks[kb])

        pending = [scores(h) for h in heads[:QK_LOOKAHEAD]]
        for n, h in enumerate(heads):
            u = pending.pop(0)
            if n + QK_LOOKAHEAD < len(heads):
                pending.append(scores(heads[n + QK_LOOKAHEAD]))
            v = v_ref[:rows, vmap[h] * LANES:(vmap[h] + 1) * LANES]
            delta = slopes[h] * key0 if slopes[h] != 0.0 else 0.0
            if mask is not None:
                u = jnp.where(mask, u, NEG)
            m_prev = m_ref[h]
            m_new = jnp.maximum(m_prev, jnp.max(u, -1, keepdims=True) + delta)
            alpha = jnp.exp2(m_prev - m_new)
            shift = m_new - delta
            psum = None
            chunks = []
            for c in range(rows // LANES):
                pc = jnp.exp2(u[:, c * LANES:(c + 1) * LANES] - shift)
                if l_ref is not None:
                    psum = pc if psum is None else psum + pc
                chunks.append(pc.astype(BF16))
            p = jnp.concatenate(chunks, axis=1)
            if l_ref is not None:
                l_ref[h] = alpha * l_ref[h] + psum
            acc_ref[h] = alpha * acc_ref[h] + _dot(p, v)
            m_ref[h] = m_new

    def normalised(h):
        acc = acc_ref[h]
        if l_ref is None:
            l = acc[:, AUX_LANE:AUX_LANE + 1]
        else:
            l = jnp.sum(l_ref[h], -1, keepdims=True)
        return acc * (1.0 / jnp.maximum(l, 1e-30))

    all_heads = list(range(n_heads))

    @pl.when(is_off)
    def _off():
        if has_sel:
            for g in range(NSA_GROUPS):
                @pl.when((flags & (STEP_GROUP0 << g)) != 0)
                def _group():
                    run_heads(all_heads[g * NSA_HPG:(g + 1) * NSA_HPG], "off")
        else:
            run_heads(all_heads, "prev" if mode == "window" else "off")

    if tk == 2 * tq:
        @pl.when(is_diag & ((flags & STEP_HIGH) == 0))
        def _diag_lo():
            run_heads(all_heads, "diag_lo")

        @pl.when(is_diag & ((flags & STEP_HIGH) != 0))
        def _diag_hi():
            run_heads(all_heads, "diag_hi")

    @pl.when(is_diag)
    def _diag():
        if tk == tq:
            run_heads(all_heads, "diag")
        if finalize == "plain":
            head_lanes = lax.broadcasted_iota(jnp.int32, (tq, LANES), 1) < HEAD_DIM
            for h in range(n_heads):
                o_ref[:, h * LANES:(h + 1) * LANES] = jnp.where(head_lanes, normalised(h), 0.0).astype(o_ref.dtype)
        else:
            lp = lam_ref[...]
            lam = (jnp.exp(jnp.sum(lp[0:1] * lp[1:2], -1, keepdims=True))
                   - jnp.exp(jnp.sum(lp[2:3] * lp[3:4], -1, keepdims=True)) + lam_init)
            for hd in range(n_heads // 2):
                o = normalised(2 * hd) - lam * normalised(2 * hd + 1)
                o = _rms_norm(o, subg_ref[...]) * (1.0 - lam_init)
                o_ref[:, hd * LANES:(hd + 1) * LANES] = o.astype(o_ref.dtype)


def _static_steps(B, nq, mode, key_ratio=1):
    qi, kj, fl = [], [], []
    for i in range(nq):
        first_j = 0 if mode == "causal" else max(i - 1, 0)
        last_j = i // key_ratio
        for j in range(first_j, last_j + 1):
            qi.append(i)
            kj.append(j)
            fl.append(STEP_ACTIVE | (STEP_FIRST if j == first_j else 0) | (STEP_DIAG if j == last_j else 0)
                      | (STEP_HIGH if j == last_j and i % key_ratio == 1 else 0))
    tile = lambda a: jnp.tile(jnp.asarray(a, jnp.int32)[None], (B, 1))
    return tile(qi), tile(kj), tile(fl)


def _selected_steps(any_sel, S, tq):
    B, nqc, G, _ = any_sel.shape
    nq = nk = S // tq
    per_tile = tq // SEL_BLOCK
    a = any_sel.reshape(B, nq, nqc // nq, G, LANES).max(axis=2) > 0
    a = a[..., :nk * per_tile].reshape(B, nq, G, nk, per_tile).any(-1)
    ii = jnp.arange(nq)[:, None]
    jj = jnp.arange(nk)[None, :]
    g_act = (a & (jj <= ii)[None, :, None, :]) | (ii == jj)[None, :, None, :]
    act = g_act.any(2)
    n_steps = nq * (nq + 1) // 2
    key = jnp.where(act, (ii * nk + jj)[None], nq * nk).reshape(B, nq * nk)
    order = jnp.sort(key, axis=1)[:, :n_steps]
    valid = order < nq * nk
    order = jnp.where(valid, order, nq * nk - 1)
    qi, kj = order // nk, order % nk
    first = valid & (qi != jnp.concatenate([jnp.full((B, 1), -1, qi.dtype), qi[:, :-1]], axis=1))
    g_bits = jnp.take_along_axis(g_act.transpose(0, 2, 1, 3).reshape(B, G, nq * nk), order[:, None, :], axis=2)
    flags = valid * STEP_ACTIVE + first * STEP_FIRST + (valid & (qi == kj)) * STEP_DIAG
    for g in range(G):
        flags = flags + (valid & g_bits[:, g]) * (STEP_GROUP0 << g)
    return qi.astype(jnp.int32), kj.astype(jnp.int32), flags.astype(jnp.int32)


def _flash_call(B, S, q_arr, q_off, n_heads, k_arr, k_off, k_blocks, v_arr, v_off, v_blocks, shared, *, kmap, vmap,
                slopes, mode, tq, name, tk=None, steps=None, sel=None, lam=None, subg=None, lam_init=0.0):
    tk = tq if tk is None else tk
    nq, nk = S // tq, S // tk
    QW, KW, VW = n_heads * LANES, k_blocks * LANES, v_blocks * LANES
    if mode == "window":
        assert tq == WINDOW
    assert tk == tq or (tk == 2 * tq and mode == "causal" and sel is None)
    if steps is None:
        steps = _static_steps(B, nq, mode, tk // tq)
    n_steps = steps[0].shape[1]
    finalize = "diff" if lam is not None else "plain"
    sum_lane = finalize == "plain"
    out_heads = n_heads // 2 if finalize == "diff" else n_heads
    in_specs = [pl.BlockSpec((tq, QW), lambda b, s, qi, kj, fl: (b * nq + qi[b, s], q_off // QW)),
                pl.BlockSpec((tk, KW), lambda b, s, qi, kj, fl: (b * nk + kj[b, s], k_off // KW)),
                pl.BlockSpec((tk, VW), lambda b, s, qi, kj, fl: (b * nk + kj[b, s], v_off // VW)),
                pl.BlockSpec((tq, tq), lambda b, s, qi, kj, fl: (0, 0)),
                pl.BlockSpec((tk, LANES), lambda b, s, qi, kj, fl: (0, 0))]
    args = [q_arr, k_arr, v_arr, shared["rel"], _key_offset_table(tk)]
    if sel is not None:
        in_specs += [pl.BlockSpec((1, NSA_GROUPS, tq, LANES), lambda b, s, qi, kj, fl: (b, 0, qi[b, s], 0)),
                     pl.BlockSpec((tk, LANES), lambda b, s, qi, kj, fl: (kj[b, s], 0))]
        args += [sel, shared["block_onehot"]]
    if finalize == "diff":
        in_specs += [pl.BlockSpec(lam.shape, lambda b, s, qi, kj, fl: (0, 0)),
                     pl.BlockSpec(subg.shape, lambda b, s, qi, kj, fl: (0, 0))]
        args += [lam, subg]
    kern = functools.partial(_flash_kernel, n_heads=n_heads, kmap=kmap, vmap=vmap, slopes=slopes, mode=mode,
                             has_sel=sel is not None, sum_lane=sum_lane, finalize=finalize, tq=tq, tk=tk,
                             lam_init=lam_init)
    stat = pltpu.VMEM((n_heads, tq, LANES), F32)
    grid_spec = pltpu.PrefetchScalarGridSpec(
        num_scalar_prefetch=3, grid=(B, n_steps), in_specs=in_specs,
        out_specs=pl.BlockSpec((tq, out_heads * LANES), lambda b, s, qi, kj, fl: (b * nq + qi[b, s], 0)),
        scratch_shapes=[stat, stat] if sum_lane else [stat, stat, stat])
    return pl.pallas_call(
        kern, grid_spec=grid_spec, out_shape=jax.ShapeDtypeStruct((B * S, out_heads * LANES), BF16),
        compiler_params=_params(("parallel", "arbitrary")), name=name)(*steps, *args)


def _combine_kernel(oc_ref, os_ref, ow_ref, ng_ref, od_ref, om_ref, mg0_ref, mg1_ref, mg2_ref, h_ref,
                    eg_ref, wn_ref, wd_ref, wm_ref, wo_ref, g_ref, b_ref, of_ref, ob_ref, *, alpha):
    sg = jax.nn.sigmoid(ng_ref[...]).astype(BF16)
    o_nsa = (_dot(sg, eg_ref[0]) * oc_ref[...].astype(F32)
             + _dot(sg, eg_ref[1]) * os_ref[...].astype(F32)
             + _dot(sg, eg_ref[2]) * ow_ref[...].astype(F32))
    y = (jax.nn.sigmoid(mg0_ref[...]) * _dot(o_nsa.astype(BF16), wn_ref[...])
         + jax.nn.sigmoid(mg1_ref[...]) * _dot(od_ref[...], wd_ref[...])
         + jax.nn.sigmoid(mg2_ref[...]) * _dot(om_ref[...], wm_ref[...]))
    mix = _dot(y.astype(BF16), wo_ref[...])
    hn = _layer_norm(alpha * h_ref[...] + mix, g_ref[...], b_ref[...])
    of_ref[...] = hn
    ob_ref[...] = hn.astype(BF16)


def _combine_call(oc, os_, ow, zf, od, om, h, eg, wn, wd, wm, wo, g, b, alpha, tm):
    T, D = h.shape

    def row(width, blk=0):
        return pl.BlockSpec((tm, width), lambda i: (i, blk))

    def full(a):
        nd = a.ndim
        return pl.BlockSpec(a.shape, lambda i: (0,) * nd)

    return pl.pallas_call(
        functools.partial(_combine_kernel, alpha=alpha), grid=(T // tm,),
        in_specs=[row(oc.shape[1]), row(os_.shape[1]), row(ow.shape[1]), row(LANES, ZF_NG // LANES),
                  row(od.shape[1]), row(om.shape[1]),
                  row(D, ZF_MG // D), row(D, ZF_MG // D + 1), row(D, ZF_MG // D + 2), row(D),
                  full(eg), full(wn), full(wd), full(wm), full(wo), full(g), full(b)],
        out_specs=[row(D), row(D)],
        out_shape=[jax.ShapeDtypeStruct((T, D), F32), jax.ShapeDtypeStruct((T, D), BF16)],
        compiler_params=_params(("parallel",)), name="mixer_combine")(
            oc, os_, ow, zf, od, om, zf, zf, zf, h, eg, wn, wd, wm, wo, g, b)


def _route(logits_t, rb):
    aff = jax.nn.sigmoid(logits_t)
    selv = aff + rb
    a_rows = [aff[e:e + 1] for e in range(N_EXPERTS)]
    s_rows = [selv[e:e + 1] for e in range(N_EXPERTS)]
    npg = EXPERTS_PER_GROUP
    best, grp = None, None
    for g in range(N_GROUPS):
        v = s_rows[g * npg:(g + 1) * npg]
        top2 = None
        for a in range(npg):
            for b in range(a + 1, npg):
                pair = v[a] + v[b]
                top2 = pair if top2 is None else jnp.maximum(top2, pair)
        if g == 0:
            best, grp = top2, jnp.zeros_like(top2, dtype=jnp.int32)
        else:
            better = top2 > best
            grp = jnp.where(better, g, grp)
            best = jnp.where(better, top2, best)

    def pick(rows, k):
        out = rows[k]
        for g in range(1, N_GROUPS):
            out = jnp.where(grp == g, rows[g * npg + k], out)
        return out

    v = [pick(s_rows, k) for k in range(npg)]
    a = [pick(a_rows, k) for k in range(npg)]
    b1, i1 = v[0], jnp.zeros_like(grp)
    for k in range(1, npg):
        gt = v[k] > b1
        i1 = jnp.where(gt, k, i1)
        b1 = jnp.where(gt, v[k], b1)
    b2, i2 = jnp.full_like(b1, -jnp.inf), jnp.zeros_like(grp)
    for k in range(npg):
        ok = (i1 != k) & (v[k] > b2)
        i2 = jnp.where(ok, k, i2)
        b2 = jnp.where(ok, v[k], b2)
    g1 = sum(jnp.where(i1 == k, a[k], 0.0) for k in range(npg))
    g2 = sum(jnp.where(i2 == k, a[k], 0.0) for k in range(npg))
    den = g1 + g2
    w1, w2 = g1 / den, g2 / den
    sub = lax.broadcasted_iota(jnp.int32, (LANES, logits_t.shape[1]), 0)
    comb = jnp.zeros(sub.shape, F32)
    for e in range(N_EXPERTS):
        g, k = divmod(e, npg)
        in_g = grp == g
        row = jnp.where(in_g & (i1 == k), w1, 0.0) + jnp.where(in_g & (i2 == k), w2, 0.0)
        comb = jnp.where(sub == e, row, comb)
    return comb, grp


GRP_LANE, POS_LANE = N_EXPERTS, N_EXPERTS + 1
MOE_CHUNK = 160


def _moe_kernel(x_ref, xb_ref, rwt_ref, rb_ref, tri_ref, w1_ref, w3_ref, w2_ref, g_ref, b_ref, of_ref, ob_ref,
                tok_ref, tok3_ref, rowv_ref, acc_ref, cnt_ref, *, alpha, tm):
    g = pl.program_id(1)

    @pl.when(g == 0)
    def _routing():
        comb_t, grp = _route(_dot_nt(rwt_ref[...], xb_ref[...]), rb_ref[...])
        sub16 = lax.broadcasted_iota(jnp.int32, (16, tm), 0)
        onehot = sub16 == grp
        ranks = _dot(onehot.astype(BF16), tri_ref[...])
        pos = jnp.sum(jnp.where(onehot, ranks, 0.0), axis=0, keepdims=True)
        grp_f = grp.astype(F32)
        for gg in range(N_GROUPS):
            cnt_ref[gg] = jnp.sum((grp == gg).astype(jnp.int32))
        rowv_ref[0:1, :] = grp_f
        rowv_ref[1:2, :] = pos
        sub = lax.broadcasted_iota(jnp.int32, (LANES, tm), 0)
        tok = jnp.where(sub == GRP_LANE, grp_f, jnp.where(sub == POS_LANE, pos, comb_t)).T
        tok_ref[...] = tok
        t1 = tok.astype(BF16)
        r1 = tok - t1.astype(F32)
        t2 = r1.astype(BF16)
        tok3_ref[0] = t1
        tok3_ref[1] = t2
        tok3_ref[2] = (r1 - t2.astype(F32)).astype(BF16)
        acc_ref[...] = jnp.zeros(acc_ref.shape, F32)

    gf = g.astype(F32)
    in_g_row = rowv_ref[0:1, :] == gf
    pos_row = rowv_ref[1:2, :]
    tok = tok_ref[...]
    in_g_col = tok[:, GRP_LANE:GRP_LANE + 1] == gf
    pos_col = tok[:, POS_LANE:POS_LANE + 1]
    r_sub = lax.broadcasted_iota(jnp.int32, (MOE_CHUNK, tm), 0).astype(F32)
    r_lane = lax.broadcasted_iota(jnp.int32, (tm, MOE_CHUNK), 1).astype(F32)
    lane = lax.broadcasted_iota(jnp.int32, (MOE_CHUNK, LANES), 1)

    def chunk(c, carry):
        base = (c * MOE_CHUNK).astype(F32)
        gather = (in_g_row & (pos_row - base == r_sub)).astype(BF16)
        scatter = (in_g_col & (pos_col - base == r_lane)).astype(BF16)
        xg = _dot(gather, xb_ref[...]).astype(BF16)
        cg = _dot(gather, tok3_ref[0]) + _dot(gather, tok3_ref[1]) + _dot(gather, tok3_ref[2])
        y = jnp.zeros((MOE_CHUNK, x_ref.shape[1]), F32)
        for k in range(EXPERTS_PER_GROUP):
            col = jnp.sum(jnp.where(lane == g * EXPERTS_PER_GROUP + k, cg, 0.0), -1, keepdims=True)
            hid = jax.nn.silu(_dot(xg, w1_ref[k])) * _dot(xg, w3_ref[k]) * col
            y = y + _dot(hid.astype(BF16), w2_ref[k])
        acc_ref[...] += _dot(scatter, y.astype(BF16))
        return carry

    lax.fori_loop(0, (cnt_ref[g] + MOE_CHUNK - 1) // MOE_CHUNK, chunk, 0)

    @pl.when(g == N_GROUPS - 1)
    def _finish():
        hn = _layer_norm(alpha * x_ref[...] + acc_ref[...], g_ref[...], b_ref[...])
        of_ref[...] = hn
        ob_ref[...] = hn.astype(BF16)


def _moe_call(h, hb, rwt, rb, w1, w3, w2, layer, g, b, alpha, tm):
    T, D = h.shape
    F = w1.shape[2]
    first_group = layer * N_GROUPS
    row = pl.BlockSpec((tm, D), lambda i, e: (i, 0))
    idx = jnp.arange(tm)
    tri = (idx[:, None] < idx[None, :]).astype(BF16)

    def full(a):
        return pl.BlockSpec(a.shape, lambda i, e: (0, 0))

    return pl.pallas_call(
        functools.partial(_moe_kernel, alpha=alpha, tm=tm), grid=(T // tm, N_GROUPS),
        in_specs=[row, row, full(rwt), full(rb), full(tri),
                  pl.BlockSpec((EXPERTS_PER_GROUP, D, F), lambda i, e: (first_group + e, 0, 0)),
                  pl.BlockSpec((EXPERTS_PER_GROUP, D, F), lambda i, e: (first_group + e, 0, 0)),
                  pl.BlockSpec((EXPERTS_PER_GROUP, F, D), lambda i, e: (first_group + e, 0, 0)), full(g), full(b)],
        out_specs=[row, row],
        out_shape=[jax.ShapeDtypeStruct((T, D), F32), jax.ShapeDtypeStruct((T, D), BF16)],
        scratch_shapes=[pltpu.VMEM((tm, LANES), F32), pltpu.VMEM((3, tm, LANES), BF16), pltpu.VMEM((8, tm), F32),
                        pltpu.VMEM((tm, D), F32), pltpu.SMEM((N_GROUPS,), jnp.int32)],
        compiler_params=_params(("parallel", "arbitrary")), name="moe")(h, hb, rwt, rb, tri, w1, w3, w2, g, b)


def _head_pad_cols(w, n_heads, width, scale=1.0):
    K = w.shape[0]
    w = (w * scale).reshape(K, n_heads, width)
    return jnp.pad(w, ((0, 0), (0, 0), (0, LANES - width))).reshape(K, n_heads * LANES)


def _head_pad_rows(w, n_heads, width):
    N = w.shape[1]
    w = w.reshape(n_heads, width, N)
    return jnp.pad(w, ((0, 0), (0, LANES - width), (0, 0))).reshape(n_heads * LANES, N)


def _rot_half_cols(w):
    half = w.shape[1] // 2
    return jnp.concatenate([-w[:, half:], w[:, :half]], axis=1)


def _in_proj_weights(w_in):
    D = w_in.shape[0]
    widths = (NSA_HEADS * HEAD_DIM,) + (NSA_GROUPS * HEAD_DIM,) * 6 + (
        3 * NSA_HEADS, DIFF_HEADS * 2 * DIFF_DIM, DIFF_HEADS * 2 * DIFF_DIM, DIFF_HEADS * 2 * DIFF_DIM,
        Q_LORA, KV_LORA, MLA_ROPE, 3 * D)
    parts, o = [], 0
    for w in widths:
        parts.append(w_in[:, o:o + w])
        o += w
    nq, kc, vc, ks, vs, kw, vw, ng, dq, dk, dv, cq, ckv, kr, mg = parts
    wb = jnp.concatenate([
        _head_pad_cols(nq, NSA_HEADS, HEAD_DIM, LOG2E * HEAD_DIM ** -0.5),
        _head_pad_cols(dq, 2 * DIFF_HEADS, DIFF_DIM, LOG2E * DIFF_DIM ** -0.5),
        _head_pad_cols(dk, 2 * DIFF_HEADS, DIFF_DIM),
        _head_pad_cols(ks, NSA_GROUPS, HEAD_DIM), _head_pad_cols(vs, NSA_GROUPS, HEAD_DIM),
        _head_pad_cols(kw, NSA_GROUPS, HEAD_DIM), _head_pad_cols(vw, NSA_GROUPS, HEAD_DIM),
        dv], axis=1).astype(BF16)

    def rope_block(w):
        return jnp.pad(w, ((0, 0), (MLA_NOPE, LANES - MLA_NOPE - MLA_ROPE)))

    wf = jnp.concatenate([
        cq, kc, vc, jnp.pad(ng, ((0, 0), (0, LANES - ng.shape[1]))), ckv,
        rope_block(kr), rope_block(_rot_half_cols(kr)), mg], axis=1).astype(BF16)
    assert wb.shape[1] == ZB_WIDTH and wf.shape[1] == ZF_WIDTH
    return wb, wf


def _bf16_terms(x, n):
    terms, rest = [], np.float32(x)
    for _ in range(n):
        t = np.float32(np.asarray(rest, dtype=BF16).astype(np.float32))
        terms.append(float(t))
        rest = np.float32(rest - t)
    return terms


def _aux_const_row(slopes):
    row = np.zeros((1, ZB_WIDTH), np.float32)
    q_blocks = [(ZB_QN + h * LANES, slopes[h]) for h in range(NSA_HEADS)]
    q_blocks += [(ZB_DQ + b * LANES, slopes[NSA_HEADS + b // 2]) for b in range(2 * DIFF_HEADS)]
    for off, slope in q_blocks:
        for n, term in enumerate(_bf16_terms(slope, ALIBI_TERMS)):
            row[0, off + AUX_LANE + 2 * n] = ALIBI_RADIX * term
            row[0, off + AUX_LANE + 2 * n + 1] = term
    for off in (ZB_VS, ZB_VW):
        for g in range(NSA_GROUPS):
            row[0, off + g * LANES + AUX_LANE] = 1.0
    return jnp.asarray(row)


def _key_offset_table(tk):
    tab = np.zeros((tk, LANES), np.float32)
    c = np.arange(tk)
    for n in range(ALIBI_TERMS):
        tab[:, AUX_LANE + 2 * n] = c // ALIBI_RADIX
        tab[:, AUX_LANE + 2 * n + 1] = c % ALIBI_RADIX
    return jnp.asarray(tab, BF16)


def _rope_tables(S):
    half = MLA_ROPE // 2
    freqs = ROPE_THETA ** (-jnp.arange(half, dtype=F32) / half)
    ang = jnp.arange(S, dtype=F32)[:, None] * freqs[None, :]
    cos = jnp.concatenate([jnp.cos(ang), jnp.cos(ang)], -1)
    sin = jnp.concatenate([jnp.sin(ang), jnp.sin(ang)], -1)
    tail = jnp.zeros((S, LANES - MLA_NOPE - MLA_ROPE), F32)
    cos_q = jnp.concatenate([jnp.ones((S, MLA_NOPE), F32), cos, tail], -1)
    cos_k = jnp.concatenate([jnp.zeros((S, MLA_NOPE), F32), cos, tail], -1)
    sin_qk = jnp.concatenate([jnp.zeros((S, MLA_NOPE), F32), sin, tail], -1)
    return cos_q, sin_qk, cos_k, sin_qk


def _score_matrix_t(S):
    nch = S // CMP_STRIDE
    ratio = CMP_LEN // CMP_STRIDE
    per_sb = SEL_BLOCK // CMP_STRIDE
    sb = jnp.arange(LANES)[:, None]
    cb = jnp.arange(nch)[None, :]
    m = jnp.zeros((LANES, nch), F32)
    for jj in range(ratio):
        chunk = cb + jj
        m = m + ((chunk // per_sb == sb) & (chunk < nch)).astype(F32)
    return m.astype(BF16)


def _cmp_flat(z, B, S):
    nch = S // CMP_STRIDE
    x = z.reshape(B, S, NSA_GROUPS, HEAD_DIM).transpose(0, 2, 1, 3).reshape(B, NSA_GROUPS, nch, CMP_STRIDE * HEAD_DIM)
    nxt = jnp.roll(x, -1, axis=2)
    return jnp.concatenate([x, nxt], -1).reshape(B * NSA_GROUPS * nch, CMP_LEN * HEAD_DIM)


def _mixer_and_ffn(h, hb, layer, B, S, p, shared, alpha):
    T, D = h.shape
    tq = min(WINDOW, S)
    wide_tk = 2 * tq if S % (2 * tq) == 0 else tq
    wb, wf = _in_proj_weights(p["w_in"])
    zb = _matmul(hb, wb, BF16, min(1024, T), ZB_WIDTH // 3, "in_proj_b", const_row=shared["aux_row"])
    zf = _matmul(hb, wf, F32, min(1024, T), ZF_WIDTH // 4, "in_proj_f")
    slopes = _alibi_slopes_log2()
    nsa_slopes, diff_slopes = slopes[:NSA_HEADS], slopes[NSA_HEADS:]

    nch = S // CMP_STRIDE
    cmp_out = []
    for off, pos, w1, w2 in ((ZF_KC, p["cmp_pos_k"], p["cmp_w1_k"], p["cmp_w2_k"]),
                             (ZF_VC, p["cmp_pos_v"], p["cmp_w1_v"], p["cmp_w2_v"])):
        flat = _cmp_flat(zf[:, off:off + NSA_GROUPS * HEAD_DIM], B, S)
        w2p = jnp.pad(w2, ((0, 0), (0, LANES - HEAD_DIM))).astype(BF16)
        out = _compress_call(flat, pos.reshape(1, CMP_LEN * HEAD_DIM), w1.astype(BF16), w2p,
                             min(512, flat.shape[0]), "nsa_compress")
        cmp_out.append(out.reshape(B, NSA_GROUPS, nch, LANES))
    o_cmp, sel, any_sel = _cmp_call(B, S, zb, cmp_out[0], cmp_out[1], shared["score_t"], min(256, S))
    grp_map = [h_ // NSA_HPG for h_ in range(NSA_HEADS)]
    o_sel = _flash_call(B, S, zb, ZB_QN, NSA_HEADS, zb, ZB_KS, NSA_GROUPS, zb, ZB_VS, NSA_GROUPS, shared,
                        kmap=grp_map, vmap=grp_map, slopes=nsa_slopes, mode="causal", tq=tq, name="nsa_sel", sel=sel,
                        steps=_selected_steps(any_sel, S, tq))
    o_win = _flash_call(B, S, zb, ZB_QN, NSA_HEADS, zb, ZB_KW, NSA_GROUPS, zb, ZB_VW, NSA_GROUPS, shared,
                        kmap=grp_map, vmap=grp_map, slopes=nsa_slopes, mode="window", tq=tq, name="nsa_win")

    lam_init = 0.8 - 0.6 * math.exp(-0.3 * layer)
    n_maps = 2 * DIFF_HEADS
    o_diff = _flash_call(B, S, zb, ZB_DQ, n_maps, zb, ZB_DK, n_maps, zb, ZB_DV, DIFF_HEADS, shared,
                         kmap=list(range(n_maps)), vmap=[m_ // 2 for m_ in range(n_maps)],
                         slopes=[diff_slopes[m_ // 2] for m_ in range(n_maps)],
                         mode="causal", tq=tq, tk=wide_tk, name="diff_attn", lam=p["diff_lambda"],
                         subg=p["diff_subln_g"].reshape(1, 2 * DIFF_DIM), lam_init=lam_init)

    w_uq = p["mla_w_uq"].reshape(Q_LORA, MLA_HEADS, MLA_NOPE + MLA_ROPE)
    wq = jnp.pad(w_uq, ((0, 0), (0, 0), (0, LANES - MLA_NOPE - MLA_ROPE))).reshape(Q_LORA, MLA_HEADS * LANES)
    rot = jnp.stack([_rot_half_cols(w_uq[:, h_, MLA_NOPE:]) for h_ in range(MLA_HEADS)], axis=1)
    wqr = jnp.pad(rot, ((0, 0), (0, 0), (MLA_NOPE, LANES - MLA_NOPE - MLA_ROPE))).reshape(Q_LORA, MLA_HEADS * LANES)
    w_ukv = p["mla_w_ukv"].reshape(KV_LORA, MLA_HEADS, MLA_NOPE + MLA_V)
    wk = _head_pad_cols(w_ukv[:, :, :MLA_NOPE].reshape(KV_LORA, -1), MLA_HEADS, MLA_NOPE)
    wv = _head_pad_cols(w_ukv[:, :, MLA_NOPE:].reshape(KV_LORA, -1), MLA_HEADS, MLA_V)
    qm, km, vm = _mla_prep_call(zf, shared["rope"], p["mla_q_norm_g"].reshape(1, Q_LORA),
                                p["mla_kv_norm_g"].reshape(1, KV_LORA), wq.astype(BF16), wqr.astype(BF16),
                                wk.astype(BF16), wv.astype(BF16), S, min(512, S))
    ident = list(range(MLA_HEADS))
    o_mla = _flash_call(B, S, qm, 0, MLA_HEADS, km, 0, MLA_HEADS, vm, 0, MLA_HEADS, shared, kmap=ident, vmap=ident,
                        slopes=[0.0] * MLA_HEADS, mode="causal", tq=tq, tk=wide_tk, name="mla_attn")

    h1, h1b = _combine_call(
        o_cmp, o_sel, o_win, zf, o_diff, o_mla, h, shared["gate_expand"],
        _head_pad_rows(p["w_br_nsa"], NSA_HEADS, HEAD_DIM).astype(BF16), p["w_br_diff"].astype(BF16),
        _head_pad_rows(p["w_br_mla"], MLA_HEADS, MLA_V).astype(BF16), p["w_out"].astype(BF16),
        p["ln1_g"].reshape(1, D), p["ln1_b"].reshape(1, D), alpha, min(256, T))

    return _moe_call(h1, h1b, shared["router_wt"], shared["router_b"], shared["moe_w1"],
                     shared["moe_w3"], shared["moe_w2"], layer, p["ln2_g"].reshape(1, D),
                     p["ln2_b"].reshape(1, D), alpha, min(512, T))


def kernel(x, ln_in_g, ln_in_b, w_in, cmp_pos_k, cmp_w1_k, cmp_w2_k, cmp_pos_v, cmp_w1_v, cmp_w2_v, diff_lambda, diff_subln_g, mla_q_norm_g, mla_kv_norm_g, mla_w_uq, mla_w_ukv, w_br_nsa, w_br_diff, w_br_mla, w_out, ln1_g, ln1_b, router_w, router_b, moe_w1, moe_w3, moe_w2, ln2_g, ln2_b):
    B, S, D = x.shape
    depth = w_in.shape[0]
    alpha = (2 * depth) ** 0.25
    T = B * S
    tq = min(WINDOW, S)
    idx = jnp.arange(tq, dtype=jnp.int32)
    gate_rows = jnp.arange(LANES)[:, None]
    gate_cols = jnp.arange(NSA_HEADS * LANES)[None, :] // LANES
    shared = {
        "rel": (idx[None, :] - idx[:, None]).astype(F32),
        "aux_row": _aux_const_row(_alibi_slopes_log2()),
        "block_onehot": ((jnp.arange(S)[:, None] // SEL_BLOCK) == jnp.arange(LANES)[None, :]).astype(BF16),
        "score_t": _score_matrix_t(S),
        "rope": _rope_tables(S),
        "gate_expand": jnp.stack([(gate_rows == gate_cols * 3 + j) for j in range(3)]).astype(BF16),
        "router_wt": router_w.T.astype(BF16),
        "router_b": router_b.reshape(N_EXPERTS, 1).astype(F32),
        "moe_w1": moe_w1.astype(BF16).reshape((depth * N_EXPERTS,) + moe_w1.shape[2:]),
        "moe_w3": moe_w3.astype(BF16).reshape((depth * N_EXPERTS,) + moe_w3.shape[2:]),
        "moe_w2": moe_w2.astype(BF16).reshape((depth * N_EXPERTS,) + moe_w2.shape[2:]),
    }
    per_layer = dict(w_in=w_in, cmp_pos_k=cmp_pos_k, cmp_w1_k=cmp_w1_k, cmp_w2_k=cmp_w2_k, cmp_pos_v=cmp_pos_v,
                     cmp_w1_v=cmp_w1_v, cmp_w2_v=cmp_w2_v, diff_lambda=diff_lambda, diff_subln_g=diff_subln_g,
                     mla_q_norm_g=mla_q_norm_g, mla_kv_norm_g=mla_kv_norm_g, mla_w_uq=mla_w_uq, mla_w_ukv=mla_w_ukv,
                     w_br_nsa=w_br_nsa, w_br_diff=w_br_diff, w_br_mla=w_br_mla, w_out=w_out, ln1_g=ln1_g,
                     ln1_b=ln1_b, ln2_g=ln2_g, ln2_b=ln2_b)
    h, hb = _ln_call(x.reshape(T, D), ln_in_g, ln_in_b, min(512, T))
    for l in range(depth):
        p = {k: v[l] for k, v in per_layer.items()}
        h, hb = _mixer_and_ffn(h, hb, l, B, S, p, shared, alpha)
    return h.reshape(B, S, D)
```

```python
import functools
import math

import numpy as np
import jax
import jax.numpy as jnp
from jax import lax
from jax.experimental import pallas as pl
from jax.experimental.pallas import tpu as pltpu

F32 = jnp.float32
BF16 = jnp.bfloat16

LANES = 128
HEAD_DIM = 64
NSA_HEADS = 8
NSA_GROUPS = 2
NSA_HPG = NSA_HEADS // NSA_GROUPS
CMP_LEN = 32
CMP_STRIDE = 16
CMP_HIDDEN = 256
SEL_BLOCK = 64
N_SELECT = 16
WINDOW = 512
FORCE_BONUS = 1.0e4
DIFF_HEADS = 4
DIFF_DIM = 64
MLA_HEADS = 8
MLA_NOPE = 64
MLA_ROPE = 32
MLA_V = 64
Q_LORA = 256
KV_LORA = 128
ROPE_THETA = 10000.0
N_EXPERTS = 16
N_GROUPS = 4
EXPERTS_PER_GROUP = N_EXPERTS // N_GROUPS
D_FF_EXPERT = 512
LN_EPS = 1e-5
RMS_EPS = 1e-6
NEG = -1e30
LOG2E = math.log2(math.e)
MASK_BIG = 2.0 ** 100
AUX_LANE = HEAD_DIM
ALIBI_TERMS = 3
ALIBI_RADIX = 16
QK_LOOKAHEAD = 2
STEP_ACTIVE, STEP_FIRST, STEP_DIAG, STEP_GROUP0, STEP_HIGH = 1, 2, 4, 8, 32

VMEM_LIMIT = 56 * 1024 * 1024

ZB_QN, ZB_DQ, ZB_DK, ZB_KS, ZB_VS, ZB_KW, ZB_VW, ZB_DV, ZB_WIDTH = 0, 1024, 2048, 3072, 3328, 3584, 3840, 4096, 4608
ZF_CQ, ZF_KC, ZF_VC, ZF_NG, ZF_CKV, ZF_KR, ZF_KRR, ZF_MG, ZF_WIDTH = 0, 256, 384, 512, 640, 768, 896, 1024, 4096


def _alibi_slopes_log2():
    n = NSA_HEADS + DIFF_HEADS
    return [LOG2E * 2.0 ** (-8.0 * i / n) for i in range(1, n + 1)]


def _params(sem):
    return pltpu.CompilerParams(dimension_semantics=sem, vmem_limit_bytes=VMEM_LIMIT)


def _layer_norm(x, g, b):
    mu = jnp.mean(x, -1, keepdims=True)
    xc = x - mu
    var = jnp.mean(xc * xc, -1, keepdims=True)
    return xc * lax.rsqrt(var + LN_EPS) * g + b


def _rms_norm(x, g):
    return x * lax.rsqrt(jnp.mean(x * x, -1, keepdims=True) + RMS_EPS) * g


def _dot(a, b):
    return jnp.dot(a, b, preferred_element_type=F32)


def _dot_nt(a, b):
    return lax.dot_general(a, b, (((1,), (1,)), ((), ())), preferred_element_type=F32)


def _ln_kernel(x_ref, g_ref, b_ref, of_ref, ob_ref):
    y = _layer_norm(x_ref[...], g_ref[...], b_ref[...])
    of_ref[...] = y
    ob_ref[...] = y.astype(BF16)


def _ln_call(x, g, b, tm):
    T, D = x.shape
    row = pl.BlockSpec((tm, D), lambda i: (i, 0))
    vec = pl.BlockSpec((1, D), lambda i: (0, 0))
    return pl.pallas_call(
        _ln_kernel, grid=(T // tm,), in_specs=[row, vec, vec], out_specs=[row, row],
        out_shape=[jax.ShapeDtypeStruct((T, D), F32), jax.ShapeDtypeStruct((T, D), BF16)],
        compiler_params=_params(("parallel",)), name="ln_in")(x, g.reshape(1, D), b.reshape(1, D))


def _mm_kernel(a_ref, w_ref, o_ref):
    o_ref[...] = _dot(a_ref[...], w_ref[...]).astype(o_ref.dtype)


def _mm_const_kernel(a_ref, w_ref, c_ref, o_ref):
    o_ref[...] = (_dot(a_ref[...], w_ref[...]) + c_ref[...]).astype(o_ref.dtype)


def _matmul(a, w, out_dtype, tm, tn, name, const_row=None):
    M, K = a.shape
    N = w.shape[1]
    in_specs = [pl.BlockSpec((tm, K), lambda i, j: (i, 0)), pl.BlockSpec((K, tn), lambda i, j: (0, j))]
    args = [a, w]
    if const_row is not None:
        in_specs.append(pl.BlockSpec((1, tn), lambda i, j: (0, j)))
        args.append(const_row)
    return pl.pallas_call(
        _mm_kernel if const_row is None else _mm_const_kernel, grid=(M // tm, N // tn), in_specs=in_specs,
        out_specs=pl.BlockSpec((tm, tn), lambda i, j: (i, j)),
        out_shape=jax.ShapeDtypeStruct((M, N), out_dtype),
        compiler_params=_params(("parallel", "arbitrary")), name=name)(*args)


def _compress_kernel(x_ref, pos_ref, w1_ref, w2_ref, o_ref):
    flat = (x_ref[...] + pos_ref[...]).astype(BF16)
    hid = jax.nn.gelu(_dot(flat, w1_ref[...]))
    o_ref[...] = _dot(hid.astype(BF16), w2_ref[...]).astype(o_ref.dtype)


def _compress_call(flat, pos_flat, w1, w2p, tm, name):
    R, W = flat.shape
    return pl.pallas_call(
        _compress_kernel, grid=(R // tm,),
        in_specs=[pl.BlockSpec((tm, W), lambda i: (i, 0)), pl.BlockSpec((1, W), lambda i: (0, 0)),
                  pl.BlockSpec(w1.shape, lambda i: (0, 0)), pl.BlockSpec(w2p.shape, lambda i: (0, 0))],
        out_specs=pl.BlockSpec((tm, LANES), lambda i: (i, 0)),
        out_shape=jax.ShapeDtypeStruct((R, LANES), BF16),
        compiler_params=_params(("parallel",)), name=name)(flat, pos_flat, w1, w2p)


def _mla_prep_kernel(cq_ref, ckv_ref, kr_ref, krr_ref, cq128_ref, sq128_ref, ck128_ref, sk128_ref,
                     qg_ref, kvg_ref, wq_ref, wqr_ref, wk_ref, wv_ref, q_out, k_out, v_out, *, scale):
    cqn = _rms_norm(cq_ref[...], qg_ref[...]).astype(BF16)
    q_main = _dot(cqn, wq_ref[...])
    q_rot = _dot(cqn, wqr_ref[...])
    ckvn = _rms_norm(ckv_ref[...], kvg_ref[...]).astype(BF16)
    k_nope = _dot(ckvn, wk_ref[...])
    lane = lax.broadcasted_iota(jnp.int32, (1, MLA_HEADS * LANES), 1)
    sum_lane = ((lane & (LANES - 1)) == AUX_LANE).astype(F32)
    v_out[...] = (_dot(ckvn, wv_ref[...]) + sum_lane).astype(BF16)
    k_rope = kr_ref[...] * ck128_ref[...] + krr_ref[...] * sk128_ref[...]
    cq128 = cq128_ref[...]
    sq128 = sq128_ref[...]
    for h in range(MLA_HEADS):
        sl = slice(h * LANES, (h + 1) * LANES)
        q_out[:, sl] = ((q_main[:, sl] * cq128 + q_rot[:, sl] * sq128) * scale).astype(BF16)
        k_out[:, sl] = (k_nope[:, sl] + k_rope).astype(BF16)


def _mla_prep_call(zf, tabs, qg, kvg, wq, wqr, wk, wv, S, tm):
    T = zf.shape[0]
    npos = S // tm
    HW = MLA_HEADS * LANES

    def col(width, off):
        return pl.BlockSpec((tm, width), lambda i: (i, off // width))

    tab = pl.BlockSpec((tm, LANES), lambda i: (i % npos, 0))

    def full(a):
        return pl.BlockSpec(a.shape, lambda i: (0, 0))

    out = pl.BlockSpec((tm, HW), lambda i: (i, 0))
    return pl.pallas_call(
        functools.partial(_mla_prep_kernel, scale=LOG2E * (MLA_NOPE + MLA_ROPE) ** -0.5), grid=(T // tm,),
        in_specs=[col(Q_LORA, ZF_CQ), col(KV_LORA, ZF_CKV), col(LANES, ZF_KR), col(LANES, ZF_KRR),
                  tab, tab, tab, tab, full(qg), full(kvg), full(wq), full(wqr), full(wk), full(wv)],
        out_specs=[out, out, out],
        out_shape=[jax.ShapeDtypeStruct((T, HW), BF16)] * 3,
        compiler_params=_params(("parallel",)), name="mla_prep")(
            zf, zf, zf, zf, *tabs, qg, kvg, wq, wqr, wk, wv)


def _cmp_kernel(q_ref, kc_ref, vc_ref, kctab_ref, mt_ref, o_ref, sel_ref, any_ref, sc_ref, *, tq, nch, nsb, nsel):
    i = pl.program_id(1)
    t0 = i * tq
    row_valid = t0 + lax.broadcasted_iota(jnp.int32, (tq, 1), 0) >= CMP_LEN - 1

    def attend(ncols):
        r = lax.broadcasted_iota(jnp.int32, (tq, ncols), 0)
        c = lax.broadcasted_iota(jnp.int32, (tq, ncols), 1)
        valid = (t0 + r >= CMP_STRIDE * c + (CMP_LEN - 1)) & (c < nch - 1)
        mt = mt_ref[:, :ncols]
        for g in range(NSA_GROUPS):
            kc = kc_ref[0, g, :ncols, :] + kctab_ref[:ncols, :]
            vc = vc_ref[0, g, :ncols, :]
            p_grp = jnp.zeros((tq, ncols), F32)
            for hh in range(NSA_HPG):
                h = g * NSA_HPG + hh
                sl = slice(h * LANES, (h + 1) * LANES)
                z = jnp.where(valid, _dot_nt(q_ref[:, sl], kc), NEG)
                e = jnp.exp2(z - jnp.max(z, -1, keepdims=True))
                inv = jnp.where(row_valid, 1.0 / jnp.maximum(jnp.sum(e, -1, keepdims=True), 1e-30), 0.0)
                p = e * inv
                o_ref[:, sl] = _dot(p.astype(BF16), vc).astype(BF16)
                p_grp = p_grp + p
            p1 = p_grp.astype(BF16)
            r1 = p_grp - p1.astype(F32)
            p2 = r1.astype(BF16)
            p3 = (r1 - p2.astype(F32)).astype(BF16)
            sc_ref[g] = _dot_nt(mt, p1) + _dot_nt(mt, p2) + _dot_nt(mt, p3)

    widths = sorted({min(nch, w) for w in range(LANES, nch + LANES, LANES)})
    ended = (t0 + tq) // CMP_STRIDE - (CMP_LEN // CMP_STRIDE - 1)
    for n, width in enumerate(widths):
        lo = widths[n - 1] if n else -1
        hi = width if n + 1 < len(widths) else nch + LANES
        @pl.when((ended > lo) & (ended <= hi))
        def _variant():
            attend(width)

    sb = lax.broadcasted_iota(jnp.int32, (LANES, tq), 0)
    sb_f = sb.astype(F32)
    t = t0 + lax.broadcasted_iota(jnp.int32, (LANES, tq), 1)
    cur = lax.shift_right_arithmetic(t, SEL_BLOCK.bit_length() - 1)
    forced = (sb == 0) | (sb == cur) | (sb == cur - 1)
    started = sb * SEL_BLOCK <= t
    for g in range(NSA_GROUPS):
        sc = sc_ref[g]
        sc = jnp.where(forced, sc + FORCE_BONUS, sc)
        sc = jnp.where(started, sc, -FORCE_BONUS)
        sc = jnp.where(sb < nsb, sc, -jnp.inf)
        sel_t = jnp.zeros((LANES, tq), F32)
        for _ in range(nsel):
            m = jnp.max(sc, axis=0, keepdims=True)
            cand = jnp.where(sc == m, sb_f, float(LANES))
            idx = jnp.min(cand, axis=0, keepdims=True)
            hit = sb_f == idx
            sel_t = jnp.where(hit, 1.0, sel_t)
            sc = jnp.where(hit, -jnp.inf, sc)
        sel_mat = sel_t.T
        sel_ref[0, g] = ((sel_mat - 1.0) * MASK_BIG).astype(BF16)
        any_ref[0, 0, g:g + 1, :] = jnp.max(sel_mat, axis=0, keepdims=True)


def _cmp_key_table(nch):
    tab = np.zeros((nch, LANES), np.float32)
    c = np.arange(nch)
    for n in range(ALIBI_TERMS):
        tab[:, AUX_LANE + 2 * n] = CMP_STRIDE * (c // ALIBI_RADIX)
        tab[:, AUX_LANE + 2 * n + 1] = CMP_STRIDE * (c % ALIBI_RADIX)
    return jnp.asarray(tab, BF16)


def _cmp_call(B, S, zb, kcmp, vcmp, mt, tq):
    nch = S // CMP_STRIDE
    nsb = S // SEL_BLOCK
    nq = S // tq
    HW = NSA_HEADS * LANES
    kern = functools.partial(_cmp_kernel, tq=tq, nch=nch, nsb=nsb, nsel=min(N_SELECT, nsb))
    cmp_spec = pl.BlockSpec((1, NSA_GROUPS, nch, LANES), lambda b, i: (b, 0, 0, 0))
    return pl.pallas_call(
        kern, grid=(B, nq),
        in_specs=[pl.BlockSpec((tq, HW), lambda b, i: (b * nq + i, ZB_QN // HW)), cmp_spec, cmp_spec,
                  pl.BlockSpec((nch, LANES), lambda b, i: (0, 0)), pl.BlockSpec(mt.shape, lambda b, i: (0, 0))],
        out_specs=[pl.BlockSpec((tq, HW), lambda b, i: (b * nq + i, 0)),
                   pl.BlockSpec((1, NSA_GROUPS, tq, LANES), lambda b, i: (b, 0, i, 0)),
                   pl.BlockSpec((1, 1, NSA_GROUPS, LANES), lambda b, i: (b, i, 0, 0))],
        out_shape=[jax.ShapeDtypeStruct((B * S, HW), BF16),
                   jax.ShapeDtypeStruct((B, NSA_GROUPS, S, LANES), BF16),
                   jax.ShapeDtypeStruct((B, nq, NSA_GROUPS, LANES), F32)],
        scratch_shapes=[pltpu.VMEM((NSA_GROUPS, LANES, tq), F32)],
        compiler_params=_params(("parallel", "parallel")), name="nsa_cmp")(zb, kcmp, vcmp, _cmp_key_table(nch), mt)


def _flash_kernel(*refs, n_heads, kmap, vmap, slopes, mode, has_sel, sum_lane, finalize, tq, tk, lam_init):
    kj_ref, fl_ref = refs[1], refs[2]
    it = iter(refs[3:])
    q_ref, k_ref, v_ref, rel_ref, ktab_ref = next(it), next(it), next(it), next(it), next(it)
    if has_sel:
        sel_ref, et_ref = next(it), next(it)
    if finalize == "diff":
        lam_ref, subg_ref = next(it), next(it)
    o_ref, m_ref, acc_ref = next(it), next(it), next(it)
    l_ref = None if sum_lane else next(it)

    b = pl.program_id(0)
    s = pl.program_id(1)
    kv = kj_ref[b, s]
    flags = fl_ref[b, s]
    is_diag = (flags & STEP_DIAG) != 0
    is_off = ((flags & STEP_ACTIVE) != 0) & jnp.logical_not(is_diag)

    @pl.when((flags & STEP_FIRST) != 0)
    def _init():
        m_ref[...] = jnp.full(m_ref.shape, NEG, F32)
        acc_ref[...] = jnp.zeros(acc_ref.shape, F32)
        if l_ref is not None:
            l_ref[...] = jnp.zeros(l_ref.shape, F32)

    def run_heads(heads, variant):
        rows = tq if variant == "diag_lo" else tk
        if variant in ("diag", "diag_lo"):
            mask = rel_ref[...] <= 0
        elif variant == "diag_hi":
            mask = jnp.concatenate([rel_ref[...] - tq, rel_ref[...]], axis=1) <= 0
        elif variant == "prev":
            mask = rel_ref[...] > 0
        else:
            mask = None
        key0 = (kv * tk).astype(F32)
        k_blocks = {}

        def scores(h):
            kb = kmap[h]
            if kb not in k_blocks:
                k = k_ref[:rows, kb * LANES:(kb + 1) * LANES]
                if slopes[h] != 0.0:
                    k = k + ktab_ref[:rows, :]
                if has_sel:
                    k = jnp.concatenate([k, et_ref[...]], axis=1)
                k_blocks[kb] = k
            q = q_ref[:, h * LANES:(h + 1) * LANES]
            if has_sel:
                q = jnp.concatenate([q, sel_ref[0, h // NSA_HPG]], axis=1)
            return _dot_nt(q, k_blocks[kb])

        pending = [scores(h) for h in heads[:QK_LOOKAHEAD]]
        for n, h in enumerate(heads):
            u = pending.pop(0)
            if n + QK_LOOKAHEAD < len(heads):
                pending.append(scores(heads[n + QK_LOOKAHEAD]))
            v = v_ref[:rows, vmap[h] * LANES:(vmap[h] + 1) * LANES]
            delta = slopes[h] * key0 if slopes[h] != 0.0 else 0.0
            if mask is not None:
                u = jnp.where(mask, u, NEG)
            m_prev = m_ref[h]
            m_new = jnp.maximum(m_prev, jnp.max(u, -1, keepdims=True) + delta)
            alpha = jnp.exp2(m_prev - m_new)
            shift = m_new - delta
            psum = None
            chunks = []
            for c in range(rows // LANES):
                pc = jnp.exp2(u[:, c * LANES:(c + 1) * LANES] - shift)
                if l_ref is not None:
                    psum = pc if psum is None else psum + pc
                chunks.append(pc.astype(BF16))
            p = jnp.concatenate(chunks, axis=1)
            if l_ref is not None:
                l_ref[h] = alpha * l_ref[h] + psum
            acc_ref[h] = alpha * acc_ref[h] + _dot(p, v)
            m_ref[h] = m_new

    def normalised(h):
        acc = acc_ref[h]
        if l_ref is None:
            l = acc[:, AUX_LANE:AUX_LANE + 1]
        else:
            l = jnp.sum(l_ref[h], -1, keepdims=True)
        return acc * (1.0 / jnp.maximum(l, 1e-30))

    all_heads = list(range(n_heads))

    @pl.when(is_off)
    def _off():
        if has_sel:
            for g in range(NSA_GROUPS):
                @pl.when((flags & (STEP_GROUP0 << g)) != 0)
                def _group():
                    run_heads(all_heads[g * NSA_HPG:(g + 1) * NSA_HPG], "off")
        else:
            run_heads(all_heads, "prev" if mode == "window" else "off")

    if tk == 2 * tq:
        @pl.when(is_diag & ((flags & STEP_HIGH) == 0))
        def _diag_lo():
            run_heads(all_heads, "diag_lo")

        @pl.when(is_diag & ((flags & STEP_HIGH) != 0))
        def _diag_hi():
            run_heads(all_heads, "diag_hi")

    @pl.when(is_diag)
    def _diag():
        if tk == tq:
            run_heads(all_heads, "diag")
        if finalize == "plain":
            head_lanes = lax.broadcasted_iota(jnp.int32, (tq, LANES), 1) < HEAD_DIM
            for h in range(n_heads):
                o_ref[:, h * LANES:(h + 1) * LANES] = jnp.where(head_lanes, normalised(h), 0.0).astype(o_ref.dtype)
        else:
            lp = lam_ref[...]
            lam = (jnp.exp(jnp.sum(lp[0:1] * lp[1:2], -1, keepdims=True))
                   - jnp.exp(jnp.sum(lp[2:3] * lp[3:4], -1, keepdims=True)) + lam_init)
            for hd in range(n_heads // 2):
                o = normalised(2 * hd) - lam * normalised(2 * hd + 1)
                o = _rms_norm(o, subg_ref[...]) * (1.0 - lam_init)
                o_ref[:, hd * LANES:(hd + 1) * LANES] = o.astype(o_ref.dtype)


def _static_steps(B, nq, mode, key_ratio=1):
    qi, kj, fl = [], [], []
    for i in range(nq):
        first_j = 0 if mode == "causal" else max(i - 1, 0)
        last_j = i // key_ratio
        for j in range(first_j, last_j + 1):
            qi.append(i)
            kj.append(j)
            fl.append(STEP_ACTIVE | (STEP_FIRST if j == first_j else 0) | (STEP_DIAG if j == last_j else 0)
                      | (STEP_HIGH if j == last_j and i % key_ratio == 1 else 0))
    tile = lambda a: jnp.tile(jnp.asarray(a, jnp.int32)[None], (B, 1))
    return tile(qi), tile(kj), tile(fl)


def _selected_steps(any_sel, S, tq):
    B, nqc, G, _ = any_sel.shape
    nq = nk = S // tq
    per_tile = tq // SEL_BLOCK
    a = any_sel.reshape(B, nq, nqc // nq, G, LANES).max(axis=2) > 0
    a = a[..., :nk * per_tile].reshape(B, nq, G, nk, per_tile).any(-1)
    ii = jnp.arange(nq)[:, None]
    jj = jnp.arange(nk)[None, :]
    g_act = (a & (jj <= ii)[None, :, None, :]) | (ii == jj)[None, :, None, :]
    act = g_act.any(2)
    n_steps = nq * (nq + 1) // 2
    key = jnp.where(act, (ii * nk + jj)[None], nq * nk).reshape(B, nq * nk)
    order = jnp.sort(key, axis=1)[:, :n_steps]
    valid = order < nq * nk
    order = jnp.where(valid, order, nq * nk - 1)
    qi, kj = order // nk, order % nk
    first = valid & (qi != jnp.concatenate([jnp.full((B, 1), -1, qi.dtype), qi[:, :-1]], axis=1))
    g_bits = jnp.take_along_axis(g_act.transpose(0, 2, 1, 3).reshape(B, G, nq * nk), order[:, None, :], axis=2)
    flags = valid * STEP_ACTIVE + first * STEP_FIRST + (valid & (qi == kj)) * STEP_DIAG
    for g in range(G):
        flags = flags + (valid & g_bits[:, g]) * (STEP_GROUP0 << g)
    return qi.astype(jnp.int32), kj.astype(jnp.int32), flags.astype(jnp.int32)


def _flash_call(B, S, q_arr, q_off, n_heads, k_arr, k_off, k_blocks, v_arr, v_off, v_blocks, shared, *, kmap, vmap,
                slopes, mode, tq, name, tk=None, steps=None, sel=None, lam=None, subg=None, lam_init=0.0):
    tk = tq if tk is None else tk
    nq, nk = S // tq, S // tk
    QW, KW, VW = n_heads * LANES, k_blocks * LANES, v_blocks * LANES
    if mode == "window":
        assert tq == WINDOW
    assert tk == tq or (tk == 2 * tq and mode == "causal" and sel is None)
    if steps is None:
        steps = _static_steps(B, nq, mode, tk // tq)
    n_steps = steps[0].shape[1]
    finalize = "diff" if lam is not None else "plain"
    sum_lane = finalize == "plain"
    out_heads = n_heads // 2 if finalize == "diff" else n_heads
    in_specs = [pl.BlockSpec((tq, QW), lambda b, s, qi, kj, fl: (b * nq + qi[b, s], q_off // QW)),
                pl.BlockSpec((tk, KW), lambda b, s, qi, kj, fl: (b * nk + kj[b, s], k_off // KW)),
                pl.BlockSpec((tk, VW), lambda b, s, qi, kj, fl: (b * nk + kj[b, s], v_off // VW)),
                pl.BlockSpec((tq, tq), lambda b, s, qi, kj, fl: (0, 0)),
                pl.BlockSpec((tk, LANES), lambda b, s, qi, kj, fl: (0, 0))]
    args = [q_arr, k_arr, v_arr, shared["rel"], _key_offset_table(tk)]
    if sel is not None:
        in_specs += [pl.BlockSpec((1, NSA_GROUPS, tq, LANES), lambda b, s, qi, kj, fl: (b, 0, qi[b, s], 0)),
                     pl.BlockSpec((tk, LANES), lambda b, s, qi, kj, fl: (kj[b, s], 0))]
        args += [sel, shared["block_onehot"]]
    if finalize == "diff":
        in_specs += [pl.BlockSpec(lam.shape, lambda b, s, qi, kj, fl: (0, 0)),
                     pl.BlockSpec(subg.shape, lambda b, s, qi, kj, fl: (0, 0))]
        args += [lam, subg]
    kern = functools.partial(_flash_kernel, n_heads=n_heads, kmap=kmap, vmap=vmap, slopes=slopes, mode=mode,
                             has_sel=sel is not None, sum_lane=sum_lane, finalize=finalize, tq=tq, tk=tk,
                             lam_init=lam_init)
    stat = pltpu.VMEM((n_heads, tq, LANES), F32)
    grid_spec = pltpu.PrefetchScalarGridSpec(
        num_scalar_prefetch=3, grid=(B, n_steps), in_specs=in_specs,
        out_specs=pl.BlockSpec((tq, out_heads * LANES), lambda b, s, qi, kj, fl: (b * nq + qi[b, s], 0)),
        scratch_shapes=[stat, stat] if sum_lane else [stat, stat, stat])
    return pl.pallas_call(
        kern, grid_spec=grid_spec, out_shape=jax.ShapeDtypeStruct((B * S, out_heads * LANES), BF16),
        compiler_params=_params(("parallel", "arbitrary")), name=name)(*steps, *args)


def _combine_kernel(oc_ref, os_ref, ow_ref, ng_ref, od_ref, om_ref, mg0_ref, mg1_ref, mg2_ref, h_ref,
                    eg_ref, wn_ref, wd_ref, wm_ref, wo_ref, g_ref, b_ref, of_ref, ob_ref, *, alpha):
    sg = jax.nn.sigmoid(ng_ref[...]).astype(BF16)
    o_nsa = (_dot(sg, eg_ref[0]) * oc_ref[...].astype(F32)
             + _dot(sg, eg_ref[1]) * os_ref[...].astype(F32)
             + _dot(sg, eg_ref[2]) * ow_ref[...].astype(F32))
    y = (jax.nn.sigmoid(mg0_ref[...]) * _dot(o_nsa.astype(BF16), wn_ref[...])
         + jax.nn.sigmoid(mg1_ref[...]) * _dot(od_ref[...], wd_ref[...])
         + jax.nn.sigmoid(mg2_ref[...]) * _dot(om_ref[...], wm_ref[...]))
    mix = _dot(y.astype(BF16), wo_ref[...])
    hn = _layer_norm(alpha * h_ref[...] + mix, g_ref[...], b_ref[...])
    of_ref[...] = hn
    ob_ref[...] = hn.astype(BF16)


def _combine_call(oc, os_, ow, zf, od, om, h, eg, wn, wd, wm, wo, g, b, alpha, tm):
    T, D = h.shape

    def row(width, blk=0):
        return pl.BlockSpec((tm, width), lambda i: (i, blk))

    def full(a):
        nd = a.ndim
        return pl.BlockSpec(a.shape, lambda i: (0,) * nd)

    return pl.pallas_call(
        functools.partial(_combine_kernel, alpha=alpha), grid=(T // tm,),
        in_specs=[row(oc.shape[1]), row(os_.shape[1]), row(ow.shape[1]), row(LANES, ZF_NG // LANES),
                  row(od.shape[1]), row(om.shape[1]),
                  row(D, ZF_MG // D), row(D, ZF_MG // D + 1), row(D, ZF_MG // D + 2), row(D),
                  full(eg), full(wn), full(wd), full(wm), full(wo), full(g), full(b)],
        out_specs=[row(D), row(D)],
        out_shape=[jax.ShapeDtypeStruct((T, D), F32), jax.ShapeDtypeStruct((T, D), BF16)],
        compiler_params=_params(("parallel",)), name="mixer_combine")(
            oc, os_, ow, zf, od, om, zf, zf, zf, h, eg, wn, wd, wm, wo, g, b)


def _route(logits_t, rb):
    aff = jax.nn.sigmoid(logits_t)
    selv = aff + rb
    a_rows = [aff[e:e + 1] for e in range(N_EXPERTS)]
    s_rows = [selv[e:e + 1] for e in range(N_EXPERTS)]
    npg = EXPERTS_PER_GROUP
    best, grp = None, None
    for g in range(N_GROUPS):
        v = s_rows[g * npg:(g + 1) * npg]
        top2 = None
        for a in range(npg):
            for b in range(a + 1, npg):
                pair = v[a] + v[b]
                top2 = pair if top2 is None else jnp.maximum(top2, pair)
        if g == 0:
            best, grp = top2, jnp.zeros_like(top2, dtype=jnp.int32)
        else:
            better = top2 > best
            grp = jnp.where(better, g, grp)
            best = jnp.where(better, top2, best)

    def pick(rows, k):
        out = rows[k]
        for g in range(1, N_GROUPS):
            out = jnp.where(grp == g, rows[g * npg + k], out)
        return out

    v = [pick(s_rows, k) for k in range(npg)]
    a = [pick(a_rows, k) for k in range(npg)]
    b1, i1 = v[0], jnp.zeros_like(grp)
    for k in range(1, npg):
        gt = v[k] > b1
        i1 = jnp.where(gt, k, i1)
        b1 = jnp.where(gt, v[k], b1)
    b2, i2 = jnp.full_like(b1, -jnp.inf), jnp.zeros_like(grp)
    for k in range(npg):
        ok = (i1 != k) & (v[k] > b2)
        i2 = jnp.where(ok, k, i2)
        b2 = jnp.where(ok, v[k], b2)
    g1 = sum(jnp.where(i1 == k, a[k], 0.0) for k in range(npg))
    g2 = sum(jnp.where(i2 == k, a[k], 0.0) for k in range(npg))
    den = g1 + g2
    w1, w2 = g1 / den, g2 / den
    sub = lax.broadcasted_iota(jnp.int32, (LANES, logits_t.shape[1]), 0)
    comb = jnp.zeros(sub.shape, F32)
    for e in range(N_EXPERTS):
        g, k = divmod(e, npg)
        in_g = grp == g
        row = jnp.where(in_g & (i1 == k), w1, 0.0) + jnp.where(in_g & (i2 == k), w2, 0.0)
        comb = jnp.where(sub == e, row, comb)
    return comb, grp


GRP_LANE, POS_LANE = N_EXPERTS, N_EXPERTS + 1
MOE_CHUNK = 160


def _moe_kernel(x_ref, xb_ref, rwt_ref, rb_ref, tri_ref, w1_ref, w3_ref, w2_ref, g_ref, b_ref, of_ref, ob_ref,
                tok_ref, tok3_ref, rowv_ref, acc_ref, cnt_ref, *, alpha, tm):
    g = pl.program_id(1)

    @pl.when(g == 0)
    def _routing():
        comb_t, grp = _route(_dot_nt(rwt_ref[...], xb_ref[...]), rb_ref[...])
        sub16 = lax.broadcasted_iota(jnp.int32, (16, tm), 0)
        onehot = sub16 == grp
        ranks = _dot(onehot.astype(BF16), tri_ref[...])
        pos = jnp.sum(jnp.where(onehot, ranks, 0.0), axis=0, keepdims=True)
        grp_f = grp.astype(F32)
        for gg in range(N_GROUPS):
            cnt_ref[gg] = jnp.sum((grp == gg).astype(jnp.int32))
        rowv_ref[0:1, :] = grp_f
        rowv_ref[1:2, :] = pos
        sub = lax.broadcasted_iota(jnp.int32, (LANES, tm), 0)
        tok = jnp.where(sub == GRP_LANE, grp_f, jnp.where(sub == POS_LANE, pos, comb_t)).T
        tok_ref[...] = tok
        t1 = tok.astype(BF16)
        r1 = tok - t1.astype(F32)
        t2 = r1.astype(BF16)
        tok3_ref[0] = t1
        tok3_ref[1] = t2
        tok3_ref[2] = (r1 - t2.astype(F32)).astype(BF16)
        acc_ref[...] = jnp.zeros(acc_ref.shape, F32)

    gf = g.astype(F32)
    in_g_row = rowv_ref[0:1, :] == gf
    pos_row = rowv_ref[1:2, :]
    tok = tok_ref[...]
    in_g_col = tok[:, GRP_LANE:GRP_LANE + 1] == gf
    pos_col = tok[:, POS_LANE:POS_LANE + 1]
    r_sub = lax.broadcasted_iota(jnp.int32, (MOE_CHUNK, tm), 0).astype(F32)
    r_lane = lax.broadcasted_iota(jnp.int32, (tm, MOE_CHUNK), 1).astype(F32)
    lane = lax.broadcasted_iota(jnp.int32, (MOE_CHUNK, LANES), 1)

    def chunk(c, carry):
        base = (c * MOE_CHUNK).astype(F32)
        gather = (in_g_row & (pos_row - base == r_sub)).astype(BF16)
        scatter = (in_g_col & (pos_col - base == r_lane)).astype(BF16)
        xg = _dot(gather, xb_ref[...]).astype(BF16)
        cg = _dot(gather, tok3_ref[0]) + _dot(gather, tok3_ref[1]) + _dot(gather, tok3_ref[2])
        y = jnp.zeros((MOE_CHUNK, x_ref.shape[1]), F32)
        for k in range(EXPERTS_PER_GROUP):
            col = jnp.sum(jnp.where(lane == g * EXPERTS_PER_GROUP + k, cg, 0.0), -1, keepdims=True)
            hid = jax.nn.silu(_dot(xg, w1_ref[k])) * _dot(xg, w3_ref[k]) * col
            y = y + _dot(hid.astype(BF16), w2_ref[k])
        acc_ref[...] += _dot(scatter, y.astype(BF16))
        return carry

    lax.fori_loop(0, (cnt_ref[g] + MOE_CHUNK - 1) // MOE_CHUNK, chunk, 0)

    @pl.when(g == N_GROUPS - 1)
    def _finish():
        hn = _layer_norm(alpha * x_ref[...] + acc_ref[...], g_ref[...], b_ref[...])
        of_ref[...] = hn
        ob_ref[...] = hn.astype(BF16)


def _moe_call(h, hb, rwt, rb, w1, w3, w2, layer, g, b, alpha, tm):
    T, D = h.shape
    F = w1.shape[2]
    first_group = layer * N_GROUPS
    row = pl.BlockSpec((tm, D), lambda i, e: (i, 0))
    idx = jnp.arange(tm)
    tri = (idx[:, None] < idx[None, :]).astype(BF16)

    def full(a):
        return pl.BlockSpec(a.shape, lambda i, e: (0, 0))

    return pl.pallas_call(
        functools.partial(_moe_kernel, alpha=alpha, tm=tm), grid=(T // tm, N_GROUPS),
        in_specs=[row, row, full(rwt), full(rb), full(tri),
                  pl.BlockSpec((EXPERTS_PER_GROUP, D, F), lambda i, e: (first_group + e, 0, 0)),
                  pl.BlockSpec((EXPERTS_PER_GROUP, D, F), lambda i, e: (first_group + e, 0, 0)),
                  pl.BlockSpec((EXPERTS_PER_GROUP, F, D), lambda i, e: (first_group + e, 0, 0)), full(g), full(b)],
        out_specs=[row, row],
        out_shape=[jax.ShapeDtypeStruct((T, D), F32), jax.ShapeDtypeStruct((T, D), BF16)],
        scratch_shapes=[pltpu.VMEM((tm, LANES), F32), pltpu.VMEM((3, tm, LANES), BF16), pltpu.VMEM((8, tm), F32),
                        pltpu.VMEM((tm, D), F32), pltpu.SMEM((N_GROUPS,), jnp.int32)],
        compiler_params=_params(("parallel", "arbitrary")), name="moe")(h, hb, rwt, rb, tri, w1, w3, w2, g, b)


def _head_pad_cols(w, n_heads, width, scale=1.0):
    K = w.shape[0]
    w = (w * scale).reshape(K, n_heads, width)
    return jnp.pad(w, ((0, 0), (0, 0), (0, LANES - width))).reshape(K, n_heads * LANES)


def _head_pad_rows(w, n_heads, width):
    N = w.shape[1]
    w = w.reshape(n_heads, width, N)
    return jnp.pad(w, ((0, 0), (0, LANES - width), (0, 0))).reshape(n_heads * LANES, N)


def _rot_half_cols(w):
    half = w.shape[1] // 2
    return jnp.concatenate([-w[:, half:], w[:, :half]], axis=1)


def _in_proj_weights(w_in):
    D = w_in.shape[0]
    widths = (NSA_HEADS * HEAD_DIM,) + (NSA_GROUPS * HEAD_DIM,) * 6 + (
        3 * NSA_HEADS, DIFF_HEADS * 2 * DIFF_DIM, DIFF_HEADS * 2 * DIFF_DIM, DIFF_HEADS * 2 * DIFF_DIM,
        Q_LORA, KV_LORA, MLA_ROPE, 3 * D)
    parts, o = [], 0
    for w in widths:
        parts.append(w_in[:, o:o + w])
        o += w
    nq, kc, vc, ks, vs, kw, vw, ng, dq, dk, dv, cq, ckv, kr, mg = parts
    wb = jnp.concatenate([
        _head_pad_cols(nq, NSA_HEADS, HEAD_DIM, LOG2E * HEAD_DIM ** -0.5),
        _head_pad_cols(dq, 2 * DIFF_HEADS, DIFF_DIM, LOG2E * DIFF_DIM ** -0.5),
        _head_pad_cols(dk, 2 * DIFF_HEADS, DIFF_DIM),
        _head_pad_cols(ks, NSA_GROUPS, HEAD_DIM), _head_pad_cols(vs, NSA_GROUPS, HEAD_DIM),
        _head_pad_cols(kw, NSA_GROUPS, HEAD_DIM), _head_pad_cols(vw, NSA_GROUPS, HEAD_DIM),
        dv], axis=1).astype(BF16)

    def rope_block(w):
        return jnp.pad(w, ((0, 0), (MLA_NOPE, LANES - MLA_NOPE - MLA_ROPE)))

    wf = jnp.concatenate([
        cq, kc, vc, jnp.pad(ng, ((0, 0), (0, LANES - ng.shape[1]))), ckv,
        rope_block(kr), rope_block(_rot_half_cols(kr)), mg], axis=1).astype(BF16)
    assert wb.shape[1] == ZB_WIDTH and wf.shape[1] == ZF_WIDTH
    return wb, wf


def _bf16_terms(x, n):
    terms, rest = [], np.float32(x)
    for _ in range(n):
        t = np.float32(np.asarray(rest, dtype=BF16).astype(np.float32))
        terms.append(float(t))
        rest = np.float32(rest - t)
    return terms


def _aux_const_row(slopes):
    row = np.zeros((1, ZB_WIDTH), np.float32)
    q_blocks = [(ZB_QN + h * LANES, slopes[h]) for h in range(NSA_HEADS)]
    q_blocks += [(ZB_DQ + b * LANES, slopes[NSA_HEADS + b // 2]) for b in range(2 * DIFF_HEADS)]
    for off, slope in q_blocks:
        for n, term in enumerate(_bf16_terms(slope, ALIBI_TERMS)):
            row[0, off + AUX_LANE + 2 * n] = ALIBI_RADIX * term
            row[0, off + AUX_LANE + 2 * n + 1] = term
    for off in (ZB_VS, ZB_VW):
        for g in range(NSA_GROUPS):
            row[0, off + g * LANES + AUX_LANE] = 1.0
    return jnp.asarray(row)


def _key_offset_table(tk):
    tab = np.zeros((tk, LANES), np.float32)
    c = np.arange(tk)
    for n in range(ALIBI_TERMS):
        tab[:, AUX_LANE + 2 * n] = c // ALIBI_RADIX
        tab[:, AUX_LANE + 2 * n + 1] = c % ALIBI_RADIX
    return jnp.asarray(tab, BF16)


def _rope_tables(S):
    half = MLA_ROPE // 2
    freqs = ROPE_THETA ** (-jnp.arange(half, dtype=F32) / half)
    ang = jnp.arange(S, dtype=F32)[:, None] * freqs[None, :]
    cos = jnp.concatenate([jnp.cos(ang), jnp.cos(ang)], -1)
    sin = jnp.concatenate([jnp.sin(ang), jnp.sin(ang)], -1)
    tail = jnp.zeros((S, LANES - MLA_NOPE - MLA_ROPE), F32)
    cos_q = jnp.concatenate([jnp.ones((S, MLA_NOPE), F32), cos, tail], -1)
    cos_k = jnp.concatenate([jnp.zeros((S, MLA_NOPE), F32), cos, tail], -1)
    sin_qk = jnp.concatenate([jnp.zeros((S, MLA_NOPE), F32), sin, tail], -1)
    return cos_q, sin_qk, cos_k, sin_qk


def _score_matrix_t(S):
    nch = S // CMP_STRIDE
    ratio = CMP_LEN // CMP_STRIDE
    per_sb = SEL_BLOCK // CMP_STRIDE
    sb = jnp.arange(LANES)[:, None]
    cb = jnp.arange(nch)[None, :]
    m = jnp.zeros((LANES, nch), F32)
    for jj in range(ratio):
        chunk = cb + jj
        m = m + ((chunk // per_sb == sb) & (chunk < nch)).astype(F32)
    return m.astype(BF16)


def _cmp_flat(z, B, S):
    nch = S // CMP_STRIDE
    x = z.reshape(B, S, NSA_GROUPS, HEAD_DIM).transpose(0, 2, 1, 3).reshape(B, NSA_GROUPS, nch, CMP_STRIDE * HEAD_DIM)
    nxt = jnp.roll(x, -1, axis=2)
    return jnp.concatenate([x, nxt], -1).reshape(B * NSA_GROUPS * nch, CMP_LEN * HEAD_DIM)


def _mixer_and_ffn(h, hb, layer, B, S, p, shared, alpha):
    T, D = h.shape
    tq = min(WINDOW, S)
    wide_tk = 2 * tq if S % (2 * tq) == 0 else tq
    wb, wf = _in_proj_weights(p["w_in"])
    zb = _matmul(hb, wb, BF16, min(1024, T), ZB_WIDTH // 3, "in_proj_b", const_row=shared["aux_row"])
    zf = _matmul(hb, wf, F32, min(1024, T), ZF_WIDTH // 4, "in_proj_f")
    slopes = _alibi_slopes_log2()
    nsa_slopes, diff_slopes = slopes[:NSA_HEADS], slopes[NSA_HEADS:]

    nch = S // CMP_STRIDE
    cmp_out = []
    for off, pos, w1, w2 in ((ZF_KC, p["cmp_pos_k"], p["cmp_w1_k"], p["cmp_w2_k"]),
                             (ZF_VC, p["cmp_pos_v"], p["cmp_w1_v"], p["cmp_w2_v"])):
        flat = _cmp_flat(zf[:, off:off + NSA_GROUPS * HEAD_DIM], B, S)
        w2p = jnp.pad(w2, ((0, 0), (0, LANES - HEAD_DIM))).astype(BF16)
        out = _compress_call(flat, pos.reshape(1, CMP_LEN * HEAD_DIM), w1.astype(BF16), w2p,
                             min(512, flat.shape[0]), "nsa_compress")
        cmp_out.append(out.reshape(B, NSA_GROUPS, nch, LANES))
    o_cmp, sel, any_sel = _cmp_call(B, S, zb, cmp_out[0], cmp_out[1], shared["score_t"], min(256, S))
    grp_map = [h_ // NSA_HPG for h_ in range(NSA_HEADS)]
    o_sel = _flash_call(B, S, zb, ZB_QN, NSA_HEADS, zb, ZB_KS, NSA_GROUPS, zb, ZB_VS, NSA_GROUPS, shared,
                        kmap=grp_map, vmap=grp_map, slopes=nsa_slopes, mode="causal", tq=tq, name="nsa_sel", sel=sel,
                        steps=_selected_steps(any_sel, S, tq))
    o_win = _flash_call(B, S, zb, ZB_QN, NSA_HEADS, zb, ZB_KW, NSA_GROUPS, zb, ZB_VW, NSA_GROUPS, shared,
                        kmap=grp_map, vmap=grp_map, slopes=nsa_slopes, mode="window", tq=tq, name="nsa_win")

    lam_init = 0.8 - 0.6 * math.exp(-0.3 * layer)
    n_maps = 2 * DIFF_HEADS
    o_diff = _flash_call(B, S, zb, ZB_DQ, n_maps, zb, ZB_DK, n_maps, zb, ZB_DV, DIFF_HEADS, shared,
                         kmap=list(range(n_maps)), vmap=[m_ // 2 for m_ in range(n_maps)],
                         slopes=[diff_slopes[m_ // 2] for m_ in range(n_maps)],
                         mode="causal", tq=tq, tk=wide_tk, name="diff_attn", lam=p["diff_lambda"],
                         subg=p["diff_subln_g"].reshape(1, 2 * DIFF_DIM), lam_init=lam_init)

    w_uq = p["mla_w_uq"].reshape(Q_LORA, MLA_HEADS, MLA_NOPE + MLA_ROPE)
    wq = jnp.pad(w_uq, ((0, 0), (0, 0), (0, LANES - MLA_NOPE - MLA_ROPE))).reshape(Q_LORA, MLA_HEADS * LANES)
    rot = jnp.stack([_rot_half_cols(w_uq[:, h_, MLA_NOPE:]) for h_ in range(MLA_HEADS)], axis=1)
    wqr = jnp.pad(rot, ((0, 0), (0, 0), (MLA_NOPE, LANES - MLA_NOPE - MLA_ROPE))).reshape(Q_LORA, MLA_HEADS * LANES)
    w_ukv = p["mla_w_ukv"].reshape(KV_LORA, MLA_HEADS, MLA_NOPE + MLA_V)
    wk = _head_pad_cols(w_ukv[:, :, :MLA_NOPE].reshape(KV_LORA, -1), MLA_HEADS, MLA_NOPE)
    wv = _head_pad_cols(w_ukv[:, :, MLA_NOPE:].reshape(KV_LORA, -1), MLA_HEADS, MLA_V)
    qm, km, vm = _mla_prep_call(zf, shared["rope"], p["mla_q_norm_g"].reshape(1, Q_LORA),
                                p["mla_kv_norm_g"].reshape(1, KV_LORA), wq.astype(BF16), wqr.astype(BF16),
                                wk.astype(BF16), wv.astype(BF16), S, min(512, S))
    ident = list(range(MLA_HEADS))
    o_mla = _flash_call(B, S, qm, 0, MLA_HEADS, km, 0, MLA_HEADS, vm, 0, MLA_HEADS, shared, kmap=ident, vmap=ident,
                        slopes=[0.0] * MLA_HEADS, mode="causal", tq=tq, tk=wide_tk, name="mla_attn")

    h1, h1b = _combine_call(
        o_cmp, o_sel, o_win, zf, o_diff, o_mla, h, shared["gate_expand"],
        _head_pad_rows(p["w_br_nsa"], NSA_HEADS, HEAD_DIM).astype(BF16), p["w_br_diff"].astype(BF16),
        _head_pad_rows(p["w_br_mla"], MLA_HEADS, MLA_V).astype(BF16), p["w_out"].astype(BF16),
        p["ln1_g"].reshape(1, D), p["ln1_b"].reshape(1, D), alpha, min(256, T))

    return _moe_call(h1, h1b, shared["router_wt"], shared["router_b"], shared["moe_w1"],
                     shared["moe_w3"], shared["moe_w2"], layer, p["ln2_g"].reshape(1, D),
                     p["ln2_b"].reshape(1, D), alpha, min(512, T))


def kernel(x, ln_in_g, ln_in_b, w_in, cmp_pos_k, cmp_w1_k, cmp_w2_k, cmp_pos_v, cmp_w1_v, cmp_w2_v, diff_lambda, diff_subln_g, mla_q_norm_g, mla_kv_norm_g, mla_w_uq, mla_w_ukv, w_br_nsa, w_br_diff, w_br_mla, w_out, ln1_g, ln1_b, router_w, router_b, moe_w1, moe_w3, moe_w2, ln2_g, ln2_b):
    B, S, D = x.shape
    depth = w_in.shape[0]
    alpha = (2 * depth) ** 0.25
    T = B * S
    tq = min(WINDOW, S)
    idx = jnp.arange(tq, dtype=jnp.int32)
    gate_rows = jnp.arange(LANES)[:, None]
    gate_cols = jnp.arange(NSA_HEADS * LANES)[None, :] // LANES
    shared = {
        "rel": (idx[None, :] - idx[:, None]).astype(F32),
        "aux_row": _aux_const_row(_alibi_slopes_log2()),
        "block_onehot": ((jnp.arange(S)[:, None] // SEL_BLOCK) == jnp.arange(LANES)[None, :]).astype(BF16),
        "score_t": _score_matrix_t(S),
        "rope": _rope_tables(S),
        "gate_expand": jnp.stack([(gate_rows == gate_cols * 3 + j) for j in range(3)]).astype(BF16),
        "router_wt": router_w.T.astype(BF16),
        "router_b": router_b.reshape(N_EXPERTS, 1).astype(F32),
        "moe_w1": moe_w1.astype(BF16).reshape((depth * N_EXPERTS,) + moe_w1.shape[2:]),
        "moe_w3": moe_w3.astype(BF16).reshape((depth * N_EXPERTS,) + moe_w3.shape[2:]),
        "moe_w2": moe_w2.astype(BF16).reshape((depth * N_EXPERTS,) + moe_w2.shape[2:]),
    }
    per_layer = dict(w_in=w_in, cmp_pos_k=cmp_pos_k, cmp_w1_k=cmp_w1_k, cmp_w2_k=cmp_w2_k, cmp_pos_v=cmp_pos_v,
                     cmp_w1_v=cmp_w1_v, cmp_w2_v=cmp_w2_v, diff_lambda=diff_lambda, diff_subln_g=diff_subln_g,
                     mla_q_norm_g=mla_q_norm_g, mla_kv_norm_g=mla_kv_norm_g, mla_w_uq=mla_w_uq, mla_w_ukv=mla_w_ukv,
                     w_br_nsa=w_br_nsa, w_br_diff=w_br_diff, w_br_mla=w_br_mla, w_out=w_out, ln1_g=ln1_g,
                     ln1_b=ln1_b, ln2_g=ln2_g, ln2_b=ln2_b)
    h, hb = _ln_call(x.reshape(T, D), ln_in_g, ln_in_b, min(512, T))
    for l in range(depth):
        p = {k: v[l] for k, v in per_layer.items()}
        h, hb = _mixer_and_ffn(h, hb, l, B, S, p, shared, alpha)
    return h.reshape(B, S, D)
```

```python
import functools
import math

import numpy as np
import jax
import jax.numpy as jnp
from jax import lax
from jax.experimental import pallas as pl
from jax.experimental.pallas import tpu as pltpu

F32 = jnp.float32
BF16 = jnp.bfloat16

LANES = 128
HEAD_DIM = 64
NSA_HEADS = 8
NSA_GROUPS = 2
NSA_HPG = NSA_HEADS // NSA_GROUPS
CMP_LEN = 32
CMP_STRIDE = 16
CMP_HIDDEN = 256
SEL_BLOCK = 64
N_SELECT = 16
WINDOW = 512
FORCE_BONUS = 1.0e4
DIFF_HEADS = 4
DIFF_DIM = 64
MLA_HEADS = 8
MLA_NOPE = 64
MLA_ROPE = 32
MLA_V = 64
Q_LORA = 256
KV_LORA = 128
ROPE_THETA = 10000.0
N_EXPERTS = 16
N_GROUPS = 4
EXPERTS_PER_GROUP = N_EXPERTS // N_GROUPS
D_FF_EXPERT = 512
LN_EPS = 1e-5
RMS_EPS = 1e-6
NEG = -1e30
LOG2E = math.log2(math.e)
MASK_BIG = 2.0 ** 100
AUX_LANE = HEAD_DIM
ALIBI_TERMS = 3
ALIBI_RADIX = 16
QK_LOOKAHEAD = 2
STEP_ACTIVE, STEP_FIRST, STEP_DIAG, STEP_GROUP0, STEP_HIGH = 1, 2, 4, 8, 32

VMEM_LIMIT = 56 * 1024 * 1024

ZB_QN, ZB_DQ, ZB_DK, ZB_KS, ZB_VS, ZB_KW, ZB_VW, ZB_DV, ZB_WIDTH = 0, 1024, 2048, 3072, 3328, 3584, 3840, 4096, 4608
ZF_CQ, ZF_KC, ZF_VC, ZF_NG, ZF_CKV, ZF_KR, ZF_KRR, ZF_MG, ZF_WIDTH = 0, 256, 384, 512, 640, 768, 896, 1024, 4096


def _alibi_slopes_log2():
    n = NSA_HEADS + DIFF_HEADS
    return [LOG2E * 2.0 ** (-8.0 * i / n) for i in range(1, n + 1)]


def _params(sem):
    return pltpu.CompilerParams(dimension_semantics=sem, vmem_limit_bytes=VMEM_LIMIT)


def _layer_norm(x, g, b):
    mu = jnp.mean(x, -1, keepdims=True)
    xc = x - mu
    var = jnp.mean(xc * xc, -1, keepdims=True)
    return xc * lax.rsqrt(var + LN_EPS) * g + b


def _rms_norm(x, g):
    return x * lax.rsqrt(jnp.mean(x * x, -1, keepdims=True) + RMS_EPS) * g


def _dot(a, b):
    return jnp.dot(a, b, preferred_element_type=F32)


def _dot_nt(a, b):
    return lax.dot_general(a, b, (((1,), (1,)), ((), ())), preferred_element_type=F32)


def _ln_kernel(x_ref, g_ref, b_ref, of_ref, ob_ref):
    y = _layer_norm(x_ref[...], g_ref[...], b_ref[...])
    of_ref[...] = y
    ob_ref[...] = y.astype(BF16)


def _ln_call(x, g, b, tm):
    T, D = x.shape
    row = pl.BlockSpec((tm, D), lambda i: (i, 0))
    vec = pl.BlockSpec((1, D), lambda i: (0, 0))
    return pl.pallas_call(
        _ln_kernel, grid=(T // tm,), in_specs=[row, vec, vec], out_specs=[row, row],
        out_shape=[jax.ShapeDtypeStruct((T, D), F32), jax.ShapeDtypeStruct((T, D), BF16)],
        compiler_params=_params(("parallel",)), name="ln_in")(x, g.reshape(1, D), b.reshape(1, D))


def _mm_kernel(a_ref, w_ref, o_ref):
    o_ref[...] = _dot(a_ref[...], w_ref[...]).astype(o_ref.dtype)


def _mm_const_kernel(a_ref, w_ref, c_ref, o_ref):
    o_ref[...] = (_dot(a_ref[...], w_ref[...]) + c_ref[...]).astype(o_ref.dtype)


def _matmul(a, w, out_dtype, tm, tn, name, const_row=None):
    M, K = a.shape
    N = w.shape[1]
    in_specs = [pl.BlockSpec((tm, K), lambda i, j: (i, 0)), pl.BlockSpec((K, tn), lambda i, j: (0, j))]
    args = [a, w]
    if const_row is not None:
        in_specs.append(pl.BlockSpec((1, tn), lambda i, j: (0, j)))
        args.append(const_row)
    return pl.pallas_call(
        _mm_kernel if const_row is None else _mm_const_kernel, grid=(M // tm, N // tn), in_specs=in_specs,
        out_specs=pl.BlockSpec((tm, tn), lambda i, j: (i, j)),
        out_shape=jax.ShapeDtypeStruct((M, N), out_dtype),
        compiler_params=_params(("parallel", "arbitrary")), name=name)(*args)


def _compress_kernel(x_ref, pos_ref, w1_ref, w2_ref, o_ref):
    flat = (x_ref[...] + pos_ref[...]).astype(BF16)
    hid = jax.nn.gelu(_dot(flat, w1_ref[...]))
    o_ref[...] = _dot(hid.astype(BF16), w2_ref[...]).astype(o_ref.dtype)


def _compress_call(flat, pos_flat, w1, w2p, tm, name):
    R, W = flat.shape
    return pl.pallas_call(
        _compress_kernel, grid=(R // tm,),
        in_specs=[pl.BlockSpec((tm, W), lambda i: (i, 0)), pl.BlockSpec((1, W), lambda i: (0, 0)),
                  pl.BlockSpec(w1.shape, lambda i: (0, 0)), pl.BlockSpec(w2p.shape, lambda i: (0, 0))],
        out_specs=pl.BlockSpec((tm, LANES), lambda i: (i, 0)),
        out_shape=jax.ShapeDtypeStruct((R, LANES), BF16),
        compiler_params=_params(("parallel",)), name=name)(flat, pos_flat, w1, w2p)


def _mla_prep_kernel(cq_ref, ckv_ref, kr_ref, krr_ref, cq128_ref, sq128_ref, ck128_ref, sk128_ref,
                     qg_ref, kvg_ref, wq_ref, wqr_ref, wk_ref, wv_ref, q_out, k_out, v_out, *, scale):
    cqn = _rms_norm(cq_ref[...], qg_ref[...]).astype(BF16)
    q_main = _dot(cqn, wq_ref[...])
    q_rot = _dot(cqn, wqr_ref[...])
    ckvn = _rms_norm(ckv_ref[...], kvg_ref[...]).astype(BF16)
    k_nope = _dot(ckvn, wk_ref[...])
    lane = lax.broadcasted_iota(jnp.int32, (1, MLA_HEADS * LANES), 1)
    sum_lane = ((lane & (LANES - 1)) == AUX_LANE).astype(F32)
    v_out[...] = (_dot(ckvn, wv_ref[...]) + sum_lane).astype(BF16)
    k_rope = kr_ref[...] * ck128_ref[...] + krr_ref[...] * sk128_ref[...]
    cq128 = cq128_ref[...]
    sq128 = sq128_ref[...]
    for h in range(MLA_HEADS):
        sl = slice(h * LANES, (h + 1) * LANES)
        q_out[:, sl] = ((q_main[:, sl] * cq128 + q_rot[:, sl] * sq128) * scale).astype(BF16)
        k_out[:, sl] = (k_nope[:, sl] + k_rope).astype(BF16)


def _mla_prep_call(zf, tabs, qg, kvg, wq, wqr, wk, wv, S, tm):
    T = zf.shape[0]
    npos = S // tm
    HW = MLA_HEADS * LANES

    def col(width, off):
        return pl.BlockSpec((tm, width), lambda i: (i, off // width))

    tab = pl.BlockSpec((tm, LANES), lambda i: (i % npos, 0))

    def full(a):
        return pl.BlockSpec(a.shape, lambda i: (0, 0))

    out = pl.BlockSpec((tm, HW), lambda i: (i, 0))
    return pl.pallas_call(
        functools.partial(_mla_prep_kernel, scale=LOG2E * (MLA_NOPE + MLA_ROPE) ** -0.5), grid=(T // tm,),
        in_specs=[col(Q_LORA, ZF_CQ), col(KV_LORA, ZF_CKV), col(LANES, ZF_KR), col(LANES, ZF_KRR),
                  tab, tab, tab, tab, full(qg), full(kvg), full(wq), full(wqr), full(wk), full(wv)],
        out_specs=[out, out, out],
        out_shape=[jax.ShapeDtypeStruct((T, HW), BF16)] * 3,
        compiler_params=_params(("parallel",)), name="mla_prep")(
            zf, zf, zf, zf, *tabs, qg, kvg, wq, wqr, wk, wv)


def _cmp_kernel(q_ref, kc_ref, vc_ref, kctab_ref, mt_ref, o_ref, sel_ref, any_ref, sc_ref, *, tq, nch, nsb, nsel):
    i = pl.program_id(1)
    t0 = i * tq
    row_valid = t0 + lax.broadcasted_iota(jnp.int32, (tq, 1), 0) >= CMP_LEN - 1

    def attend(ncols):
        r = lax.broadcasted_iota(jnp.int32, (tq, ncols), 0)
        c = lax.broadcasted_iota(jnp.int32, (tq, ncols), 1)
        valid = (t0 + r >= CMP_STRIDE * c + (CMP_LEN - 1)) & (c < nch - 1)
        mt = mt_ref[:, :ncols]
        for g in range(NSA_GROUPS):
            kc = kc_ref[0, g, :ncols, :] + kctab_ref[:ncols, :]
            vc = vc_ref[0, g, :ncols, :]
            p_grp = jnp.zeros((tq, ncols), F32)
            for hh in range(NSA_HPG):
                h = g * NSA_HPG + hh
                sl = slice(h * LANES, (h + 1) * LANES)
                z = jnp.where(valid, _dot_nt(q_ref[:, sl], kc), NEG)
                e = jnp.exp2(z - jnp.max(z, -1, keepdims=True))
                inv = jnp.where(row_valid, 1.0 / jnp.maximum(jnp.sum(e, -1, keepdims=True), 1e-30), 0.0)
                p = e * inv
                o_ref[:, sl] = _dot(p.astype(BF16), vc).astype(BF16)
                p_grp = p_grp + p
            p1 = p_grp.astype(BF16)
            r1 = p_grp - p1.astype(F32)
            p2 = r1.astype(BF16)
            p3 = (r1 - p2.astype(F32)).astype(BF16)
            sc_ref[g] = _dot_nt(mt, p1) + _dot_nt(mt, p2) + _dot_nt(mt, p3)

    widths = sorted({min(nch, w) for w in range(LANES, nch + LANES, LANES)})
    ended = (t0 + tq) // CMP_STRIDE - (CMP_LEN // CMP_STRIDE - 1)
    for n, width in enumerate(widths):
        lo = widths[n - 1] if n else -1
        hi = width if n + 1 < len(widths) else nch + LANES
        @pl.when((ended > lo) & (ended <= hi))
        def _variant():
            attend(width)

    sb = lax.broadcasted_iota(jnp.int32, (LANES, tq), 0)
    sb_f = sb.astype(F32)
    t = t0 + lax.broadcasted_iota(jnp.int32, (LANES, tq), 1)
    cur = lax.shift_right_arithmetic(t, SEL_BLOCK.bit_length() - 1)
    forced = (sb == 0) | (sb == cur) | (sb == cur - 1)
    started = sb * SEL_BLOCK <= t
    for g in range(NSA_GROUPS):
        sc = sc_ref[g]
        sc = jnp.where(forced, sc + FORCE_BONUS, sc)
        sc = jnp.where(started, sc, -FORCE_BONUS)
        sc = jnp.where(sb < nsb, sc, -jnp.inf)
        sel_t = jnp.zeros((LANES, tq), F32)
        for _ in range(nsel):
            m = jnp.max(sc, axis=0, keepdims=True)
            cand = jnp.where(sc == m, sb_f, float(LANES))
            idx = jnp.min(cand, axis=0, keepdims=True)
            hit = sb_f == idx
            sel_t = jnp.where(hit, 1.0, sel_t)
            sc = jnp.where(hit, -jnp.inf, sc)
        sel_mat = sel_t.T
        sel_ref[0, g] = ((sel_mat - 1.0) * MASK_BIG).astype(BF16)
        any_ref[0, 0, g:g + 1, :] = jnp.max(sel_mat, axis=0, keepdims=True)


def _cmp_key_table(nch):
    tab = np.zeros((nch, LANES), np.float32)
    c = np.arange(nch)
    for n in range(ALIBI_TERMS):
        tab[:, AUX_LANE + 2 * n] = CMP_STRIDE * (c // ALIBI_RADIX)
        tab[:, AUX_LANE + 2 * n + 1] = CMP_STRIDE * (c % ALIBI_RADIX)
    return jnp.asarray(tab, BF16)


def _cmp_call(B, S, zb, kcmp, vcmp, mt, tq):
    nch = S // CMP_STRIDE
    nsb = S // SEL_BLOCK
    nq = S // tq
    HW = NSA_HEADS * LANES
    kern = functools.partial(_cmp_kernel, tq=tq, nch=nch, nsb=nsb, nsel=min(N_SELECT, nsb))
    cmp_spec = pl.BlockSpec((1, NSA_GROUPS, nch, LANES), lambda b, i: (b, 0, 0, 0))
    return pl.pallas_call(
        kern, grid=(B, nq),
        in_specs=[pl.BlockSpec((tq, HW), lambda b, i: (b * nq + i, ZB_QN // HW)), cmp_spec, cmp_spec,
                  pl.BlockSpec((nch, LANES), lambda b, i: (0, 0)), pl.BlockSpec(mt.shape, lambda b, i: (0, 0))],
        out_specs=[pl.BlockSpec((tq, HW), lambda b, i: (b * nq + i, 0)),
                   pl.BlockSpec((1, NSA_GROUPS, tq, LANES), lambda b, i: (b, 0, i, 0)),
                   pl.BlockSpec((1, 1, NSA_GROUPS, LANES), lambda b, i: (b, i, 0, 0))],
        out_shape=[jax.ShapeDtypeStruct((B * S, HW), BF16),
                   jax.ShapeDtypeStruct((B, NSA_GROUPS, S, LANES), BF16),
                   jax.ShapeDtypeStruct((B, nq, NSA_GROUPS, LANES), F32)],
        scratch_shapes=[pltpu.VMEM((NSA_GROUPS, LANES, tq), F32)],
        compiler_params=_params(("parallel", "parallel")), name="nsa_cmp")(zb, kcmp, vcmp, _cmp_key_table(nch), mt)


def _flash_kernel(*refs, n_heads, kmap, vmap, slopes, mode, has_sel, sum_lane, finalize, tq, tk, lam_init):
    kj_ref, fl_ref = refs[1], refs[2]
    it = iter(refs[3:])
    q_ref, k_ref, v_ref, rel_ref, ktab_ref = next(it), next(it), next(it), next(it), next(it)
    if has_sel:
        sel_ref, et_ref = next(it), next(it)
    if finalize == "diff":
        lam_ref, subg_ref = next(it), next(it)
    o_ref, m_ref, acc_ref = next(it), next(it), next(it)
    l_ref = None if sum_lane else next(it)

    b = pl.program_id(0)
    s = pl.program_id(1)
    kv = kj_ref[b, s]
    flags = fl_ref[b, s]
    is_diag = (flags & STEP_DIAG) != 0
    is_off = ((flags & STEP_ACTIVE) != 0) & jnp.logical_not(is_diag)

    @pl.when((flags & STEP_FIRST) != 0)
    def _init():
        m_ref[...] = jnp.full(m_ref.shape, NEG, F32)
        acc_ref[...] = jnp.zeros(acc_ref.shape, F32)
        if l_ref is not None:
            l_ref[...] = jnp.zeros(l_ref.shape, F32)

    def run_heads(heads, variant):
        rows = tq if variant == "diag_lo" else tk
        if variant in ("diag", "diag_lo"):
            mask = rel_ref[...] <= 0
        elif variant == "diag_hi":
            mask = jnp.concatenate([rel_ref[...] - tq, rel_ref[...]], axis=1) <= 0
        elif variant == "prev":
            mask = rel_ref[...] > 0
        else:
            mask = None
        key0 = (kv * tk).astype(F32)
        k_blocks = {}

        def scores(h):
            kb = kmap[h]
            if kb not in k_blocks:
                k = k_ref[:rows, kb * LANES:(kb + 1) * LANES]
                if slopes[h] != 0.0:
                    k = k + ktab_ref[:rows, :]
                if has_sel:
                    k = jnp.concatenate([k, et_ref[...]], axis=1)
                k_blocks[kb] = k
            q = q_ref[:, h * LANES:(h + 1) * LANES]
            if has_sel:
                q = jnp.concatenate([q, sel_ref[0, h // NSA_HPG]], axis=1)
            return _dot_nt(q, k_blocks[kb])

        pending = [scores(h) for h in heads[:QK_LOOKAHEAD]]
        for n, h in enumerate(heads):
            u = pending.pop(0)
            if n + QK_LOOKAHEAD < len(heads):
                pending.append(scores(heads[n + QK_LOOKAHEAD]))
            v = v_ref[:rows, vmap[h] * LANES:(vmap[h] + 1) * LANES]
            delta = slopes[h] * key0 if slopes[h] != 0.0 else 0.0
            if mask is not None:
                u = jnp.where(mask, u, NEG)
            m_prev = m_ref[h]
            m_new = jnp.maximum(m_prev, jnp.max(u, -1, keepdims=True) + delta)
            alpha = jnp.exp2(m_prev - m_new)
            shift = m_new - delta
            psum = None
            chunks = []
            for c in range(rows // LANES):
                pc = jnp.exp2(u[:, c * LANES:(c + 1) * LANES] - shift)
                if l_ref is not None:
                    psum = pc if psum is None else psum + pc
                chunks.append(pc.astype(BF16))
            p = jnp.concatenate(chunks, axis=1)
            if l_ref is not None:
                l_ref[h] = alpha * l_ref[h] + psum
            acc_ref[h] = alpha * acc_ref[h] + _dot(p, v)
            m_ref[h] = m_new

    def normalised(h):
        acc = acc_ref[h]
        if l_ref is None:
            l = acc[:, AUX_LANE:AUX_LANE + 1]
        else:
            l = jnp.sum(l_ref[h], -1, keepdims=True)
        return acc * (1.0 / jnp.maximum(l, 1e-30))

    all_heads = list(range(n_heads))

    @pl.when(is_off)
    def _off():
        if has_sel:
            for g in range(NSA_GROUPS):
                @pl.when((flags & (STEP_GROUP0 << g)) != 0)
                def _group():
                    run_heads(all_heads[g * NSA_HPG:(g + 1) * NSA_HPG], "off")
        else:
            run_heads(all_heads, "prev" if mode == "window" else "off")

    if tk == 2 * tq:
        @pl.when(is_diag & ((flags & STEP_HIGH) == 0))
        def _diag_lo():
            run_heads(all_heads, "diag_lo")

        @pl.when(is_diag & ((flags & STEP_HIGH) != 0))
        def _diag_hi():
            run_heads(all_heads, "diag_hi")

    @pl.when(is_diag)
    def _diag():
        if tk == tq:
            run_heads(all_heads, "diag")
        if finalize == "plain":
            head_lanes = lax.broadcasted_iota(jnp.int32, (tq, LANES), 1) < HEAD_DIM
            for h in range(n_heads):
                o_ref[:, h * LANES:(h + 1) * LANES] = jnp.where(head_lanes, normalised(h), 0.0).astype(o_ref.dtype)
        else:
            lp = lam_ref[...]
            lam = (jnp.exp(jnp.sum(lp[0:1] * lp[1:2], -1, keepdims=True))
                   - jnp.exp(jnp.sum(lp[2:3] * lp[3:4], -1, keepdims=True)) + lam_init)
            for hd in range(n_heads // 2):
                o = normalised(2 * hd) - lam * normalised(2 * hd + 1)
                o = _rms_norm(o, subg_ref[...]) * (1.0 - lam_init)
                o_ref[:, hd * LANES:(hd + 1) * LANES] = o.astype(o_ref.dtype)


def _static_steps(B, nq, mode, key_ratio=1):
    qi, kj, fl = [], [], []
    for i in range(nq):
        first_j = 0 if mode == "causal" else max(i - 1, 0)
        last_j = i // key_ratio
        for j in range(first_j, last_j + 1):
            qi.append(i)
            kj.append(j)
            fl.append(STEP_ACTIVE | (STEP_FIRST if j == first_j else 0) | (STEP_DIAG if j == last_j else 0)
                      | (STEP_HIGH if j == last_j and i % key_ratio == 1 else 0))
    tile = lambda a: jnp.tile(jnp.asarray(a, jnp.int32)[None], (B, 1))
    return tile(qi), tile(kj), tile(fl)


def _selected_steps(any_sel, S, tq):
    B, nqc, G, _ = any_sel.shape
    nq = nk = S // tq
    per_tile = tq // SEL_BLOCK
    a = any_sel.reshape(B, nq, nqc // nq, G, LANES).max(axis=2) > 0
    a = a[..., :nk * per_tile].reshape(B, nq, G, nk, per_tile).any(-1)
    ii = jnp.arange(nq)[:, None]
    jj = jnp.arange(nk)[None, :]
    g_act = (a & (jj <= ii)[None, :, None, :]) | (ii == jj)[None, :, None, :]
    act = g_act.any(2)
    n_steps = nq * (nq + 1) // 2
    key = jnp.where(act, (ii * nk + jj)[None], nq * nk).reshape(B, nq * nk)
    order = jnp.sort(key, axis=1)[:, :n_steps]
    valid = order < nq * nk
    order = jnp.where(valid, order, nq * nk - 1)
    qi, kj = order // nk, order % nk
    first = valid & (qi != jnp.concatenate([jnp.full((B, 1), -1, qi.dtype), qi[:, :-1]], axis=1))
    g_bits = jnp.take_along_axis(g_act.transpose(0, 2, 1, 3).reshape(B, G, nq * nk), order[:, None, :], axis=2)
    flags = valid * STEP_ACTIVE + first * STEP_FIRST + (valid & (qi == kj)) * STEP_DIAG
    for g in range(G):
        flags = flags + (valid & g_bits[:, g]) * (STEP_GROUP0 << g)
    return qi.astype(jnp.int32), kj.astype(jnp.int32), flags.astype(jnp.int32)


def _flash_call(B, S, q_arr, q_off, n_heads, k_arr, k_off, k_blocks, v_arr, v_off, v_blocks, shared, *, kmap, vmap,
                slopes, mode, tq, name, tk=None, steps=None, sel=None, lam=None, subg=None, lam_init=0.0):
    tk = tq if tk is None else tk
    nq, nk = S // tq, S // tk
    QW, KW, VW = n_heads * LANES, k_blocks * LANES, v_blocks * LANES
    if mode == "window":
        assert tq == WINDOW
    assert tk == tq or (tk == 2 * tq and mode == "causal" and sel is None)
    if steps is None:
        steps = _static_steps(B, nq, mode, tk // tq)
    n_steps = steps[0].shape[1]
    finalize = "diff" if lam is not None else "plain"
    sum_lane = finalize == "plain"
    out_heads = n_heads // 2 if finalize == "diff" else n_heads
    in_specs = [pl.BlockSpec((tq, QW), lambda b, s, qi, kj, fl: (b * nq + qi[b, s], q_off // QW)),
                pl.BlockSpec((tk, KW), lambda b, s, qi, kj, fl: (b * nk + kj[b, s], k_off // KW)),
                pl.BlockSpec((tk, VW), lambda b, s, qi, kj, fl: (b * nk + kj[b, s], v_off // VW)),
                pl.BlockSpec((tq, tq), lambda b, s, qi, kj, fl: (0, 0)),
                pl.BlockSpec((tk, LANES), lambda b, s, qi, kj, fl: (0, 0))]
    args = [q_arr, k_arr, v_arr, shared["rel"], _key_offset_table(tk)]
    if sel is not None:
        in_specs += [pl.BlockSpec((1, NSA_GROUPS, tq, LANES), lambda b, s, qi, kj, fl: (b, 0, qi[b, s], 0)),
                     pl.BlockSpec((tk, LANES), lambda b, s, qi, kj, fl: (kj[b, s], 0))]
        args += [sel, shared["block_onehot"]]
    if finalize == "diff":
        in_specs += [pl.BlockSpec(lam.shape, lambda b, s, qi, kj, fl: (0, 0)),
                     pl.BlockSpec(subg.shape, lambda b, s, qi, kj, fl: (0, 0))]
        args += [lam, subg]
    kern = functools.partial(_flash_kernel, n_heads=n_heads, kmap=kmap, vmap=vmap, slopes=slopes, mode=mode,
                             has_sel=sel is not None, sum_lane=sum_lane, finalize=finalize, tq=tq, tk=tk,
                             lam_init=lam_init)
    stat = pltpu.VMEM((n_heads, tq, LANES), F32)
    grid_spec = pltpu.PrefetchScalarGridSpec(
        num_scalar_prefetch=3, grid=(B, n_steps), in_specs=in_specs,
        out_specs=pl.BlockSpec((tq, out_heads * LANES), lambda b, s, qi, kj, fl: (b * nq + qi[b, s], 0)),
        scratch_shapes=[stat, stat] if sum_lane else [stat, stat, stat])
    return pl.pallas_call(
        kern, grid_spec=grid_spec, out_shape=jax.ShapeDtypeStruct((B * S, out_heads * LANES), BF16),
        compiler_params=_params(("parallel", "arbitrary")), name=name)(*steps, *args)


def _combine_kernel(oc_ref, os_ref, ow_ref, ng_ref, od_ref, om_ref, mg0_ref, mg1_ref, mg2_ref, h_ref,
                    eg_ref, wn_ref, wd_ref, wm_ref, wo_ref, g_ref, b_ref, of_ref, ob_ref, *, alpha):
    sg = jax.nn.sigmoid(ng_ref[...]).astype(BF16)
    o_nsa = (_dot(sg, eg_ref[0]) * oc_ref[...].astype(F32)
             + _dot(sg, eg_ref[1]) * os_ref[...].astype(F32)
             + _dot(sg, eg_ref[2]) * ow_ref[...].astype(F32))
    y = (jax.nn.sigmoid(mg0_ref[...]) * _dot(o_nsa.astype(BF16), wn_ref[...])
         + jax.nn.sigmoid(mg1_ref[...]) * _dot(od_ref[...], wd_ref[...])
         + jax.nn.sigmoid(mg2_ref[...]) * _dot(om_ref[...], wm_ref[...]))
    mix = _dot(y.astype(BF16), wo_ref[...])
    hn = _layer_norm(alpha * h_ref[...] + mix, g_ref[...], b_ref[...])
    of_ref[...] = hn
    ob_ref[...] = hn.astype(BF16)


def _combine_call(oc, os_, ow, zf, od, om, h, eg, wn, wd, wm, wo, g, b, alpha, tm):
    T, D = h.shape

    def row(width, blk=0):
        return pl.BlockSpec((tm, width), lambda i: (i, blk))

    def full(a):
        nd = a.ndim
        return pl.BlockSpec(a.shape, lambda i: (0,) * nd)

    return pl.pallas_call(
        functools.partial(_combine_kernel, alpha=alpha), grid=(T // tm,),
        in_specs=[row(oc.shape[1]), row(os_.shape[1]), row(ow.shape[1]), row(LANES, ZF_NG // LANES),
                  row(od.shape[1]), row(om.shape[1]),
                  row(D, ZF_MG // D), row(D, ZF_MG // D + 1), row(D, ZF_MG // D + 2), row(D),
                  full(eg), full(wn), full(wd), full(wm), full(wo), full(g), full(b)],
        out_specs=[row(D), row(D)],
        out_shape=[jax.ShapeDtypeStruct((T, D), F32), jax.ShapeDtypeStruct((T, D), BF16)],
        compiler_params=_params(("parallel",)), name="mixer_combine")(
            oc, os_, ow, zf, od, om, zf, zf, zf, h, eg, wn, wd, wm, wo, g, b)


def _route(logits_t, rb):
    aff = jax.nn.sigmoid(logits_t)
    selv = aff + rb
    a_rows = [aff[e:e + 1] for e in range(N_EXPERTS)]
    s_rows = [selv[e:e + 1] for e in range(N_EXPERTS)]
    npg = EXPERTS_PER_GROUP
    best, grp = None, None
    for g in range(N_GROUPS):
        v = s_rows[g * npg:(g + 1) * npg]
        top2 = None
        for a in range(npg):
            for b in range(a + 1, npg):
                pair = v[a] + v[b]
                top2 = pair if top2 is None else jnp.maximum(top2, pair)
        if g == 0:
            best, grp = top2, jnp.zeros_like(top2, dtype=jnp.int32)
        else:
            better = top2 > best
            grp = jnp.where(better, g, grp)
            best = jnp.where(better, top2, best)

    def pick(rows, k):
        out = rows[k]
        for g in range(1, N_GROUPS):
            out = jnp.where(grp == g, rows[g * npg + k], out)
        return out

    v = [pick(s_rows, k) for k in range(npg)]
    a = [pick(a_rows, k) for k in range(npg)]
    b1, i1 = v[0], jnp.zeros_like(grp)
    for k in range(1, npg):
        gt = v[k] > b1
        i1 = jnp.where(gt, k, i1)
        b1 = jnp.where(gt, v[k], b1)
    b2, i2 = jnp.full_like(b1, -jnp.inf), jnp.zeros_like(grp)
    for k in range(npg):
        ok = (i1 != k) & (v[k] > b2)
        i2 = jnp.where(ok, k, i2)
        b2 = jnp.where(ok, v[k], b2)
    g1 = sum(jnp.where(i1 == k, a[k], 0.0) for k in range(npg))
    g2 = sum(jnp.where(i2 == k, a[k], 0.0) for k in range(npg))
    den = g1 + g2
    w1, w2 = g1 / den, g2 / den
    sub = lax.broadcasted_iota(jnp.int32, (LANES, logits_t.shape[1]), 0)
    comb = jnp.zeros(sub.shape, F32)
    for e in range(N_EXPERTS):
        g, k = divmod(e, npg)
        in_g = grp == g
        row = jnp.where(in_g & (i1 == k), w1, 0.0) + jnp.where(in_g & (i2 == k), w2, 0.0)
        comb = jnp.where(sub == e, row, comb)
    return comb, grp


GRP_LANE, POS_LANE = N_EXPERTS, N_EXPERTS + 1
MOE_CHUNK = 160


def _moe_kernel(x_ref, xb_ref, rwt_ref, rb_ref, tri_ref, w1_ref, w3_ref, w2_ref, g_ref, b_ref, of_ref, ob_ref,
                tok_all, tok3_all, rowv_all, acc_all, cnt_ref, *, alpha, tm):
    g = pl.program_id(1)
    half = pl.program_id(2)
    tok_ref, tok3_ref, rowv_ref, acc_ref = tok_all.at[half], tok3_all.at[half], rowv_all.at[half], acc_all.at[half]
    xb_tile = xb_ref.at[pl.ds(pl.multiple_of(half * tm, tm), tm), :]

    @pl.when(g == 0)
    def _routing():
        comb_t, grp = _route(_dot_nt(rwt_ref[...], xb_tile[...]), rb_ref[...])
        sub16 = lax.broadcasted_iota(jnp.int32, (16, tm), 0)
        onehot = sub16 == grp
        ranks = _dot(onehot.astype(BF16), tri_ref[...])
        pos = jnp.sum(jnp.where(onehot, ranks, 0.0), axis=0, keepdims=True)
        grp_f = grp.astype(F32)
        for gg in range(N_GROUPS):
            cnt_ref[half * N_GROUPS + gg] = jnp.sum((grp == gg).astype(jnp.int32))
        rowv_ref[0:1, :] = grp_f
        rowv_ref[1:2, :] = pos
        sub = lax.broadcasted_iota(jnp.int32, (LANES, tm), 0)
        tok = jnp.where(sub == GRP_LANE, grp_f, jnp.where(sub == POS_LANE, pos, comb_t)).T
        tok_ref[...] = tok
        t1 = tok.astype(BF16)
        r1 = tok - t1.astype(F32)
        t2 = r1.astype(BF16)
        tok3_ref[0] = t1
        tok3_ref[1] = t2
        tok3_ref[2] = (r1 - t2.astype(F32)).astype(BF16)
        acc_ref[...] = jnp.zeros(acc_ref.shape, F32)

    gf = g.astype(F32)
    in_g_row = rowv_ref[0:1, :] == gf
    pos_row = rowv_ref[1:2, :]
    tok = tok_ref[...]
    in_g_col = tok[:, GRP_LANE:GRP_LANE + 1] == gf
    pos_col = tok[:, POS_LANE:POS_LANE + 1]
    r_sub = lax.broadcasted_iota(jnp.int32, (MOE_CHUNK, tm), 0).astype(F32)
    r_lane = lax.broadcasted_iota(jnp.int32, (tm, MOE_CHUNK), 1).astype(F32)
    lane = lax.broadcasted_iota(jnp.int32, (MOE_CHUNK, LANES), 1)

    def chunk(c, carry):
        base = (c * MOE_CHUNK).astype(F32)
        gather = (in_g_row & (pos_row - base == r_sub)).astype(BF16)
        scatter = (in_g_col & (pos_col - base == r_lane)).astype(BF16)
        xg = _dot(gather, xb_tile[...]).astype(BF16)
        cg = _dot(gather, tok3_ref[0]) + _dot(gather, tok3_ref[1]) + _dot(gather, tok3_ref[2])
        y = jnp.zeros((MOE_CHUNK, x_ref.shape[1]), F32)
        for k in range(EXPERTS_PER_GROUP):
            col = jnp.sum(jnp.where(lane == g * EXPERTS_PER_GROUP + k, cg, 0.0), -1, keepdims=True)
            hid = jax.nn.silu(_dot(xg, w1_ref[k])) * _dot(xg, w3_ref[k]) * col
            y = y + _dot(hid.astype(BF16), w2_ref[k])
        acc_ref[...] += _dot(scatter, y.astype(BF16))
        return carry

    lax.fori_loop(0, (cnt_ref[half * N_GROUPS + g] + MOE_CHUNK - 1) // MOE_CHUNK, chunk, 0)

    @pl.when(g == N_GROUPS - 1)
    def _finish():
        hn = _layer_norm(alpha * x_ref[...] + acc_ref[...], g_ref[...], b_ref[...])
        of_ref[...] = hn
        ob_ref[...] = hn.astype(BF16)


def _moe_call(h, hb, rwt, rb, w1, w3, w2, layer, g, b, alpha, tm):
    T, D = h.shape
    F = w1.shape[2]
    first_group = layer * N_GROUPS
    pair = 2 if T % (2 * tm) == 0 else 1
    last_row = pl.BlockSpec((tm, D), lambda o, e, hf: (o * pair + jnp.where(e == N_GROUPS - 1, hf, 0), 0))
    idx = jnp.arange(tm)
    tri = (idx[:, None] < idx[None, :]).astype(BF16)

    def full(a):
        return pl.BlockSpec(a.shape, lambda o, e, hf: (0, 0))

    def weights(rows, cols):
        return pl.BlockSpec((EXPERTS_PER_GROUP, rows, cols), lambda o, e, hf: (first_group + e, 0, 0))

    return pl.pallas_call(
        functools.partial(_moe_kernel, alpha=alpha, tm=tm), grid=(T // (pair * tm), N_GROUPS, pair),
        in_specs=[last_row, pl.BlockSpec((pair * tm, D), lambda o, e, hf: (o, 0)), full(rwt), full(rb), full(tri),
                  weights(D, F), weights(D, F), weights(F, D), full(g), full(b)],
        out_specs=[last_row, last_row],
        out_shape=[jax.ShapeDtypeStruct((T, D), F32), jax.ShapeDtypeStruct((T, D), BF16)],
        scratch_shapes=[pltpu.VMEM((pair, tm, LANES), F32), pltpu.VMEM((pair, 3, tm, LANES), BF16),
                        pltpu.VMEM((pair, 8, tm), F32), pltpu.VMEM((pair, tm, D), F32),
                        pltpu.SMEM((pair * N_GROUPS,), jnp.int32)],
        compiler_params=_params(("parallel", "arbitrary", "arbitrary")), name="moe")(
            h, hb, rwt, rb, tri, w1, w3, w2, g, b)


def _head_pad_cols(w, n_heads, width, scale=1.0):
    K = w.shape[0]
    w = (w * scale).reshape(K, n_heads, width)
    return jnp.pad(w, ((0, 0), (0, 0), (0, LANES - width))).reshape(K, n_heads * LANES)


def _head_pad_rows(w, n_heads, width):
    N = w.shape[1]
    w = w.reshape(n_heads, width, N)
    return jnp.pad(w, ((0, 0), (0, LANES - width), (0, 0))).reshape(n_heads * LANES, N)


def _rot_half_cols(w):
    half = w.shape[1] // 2
    return jnp.concatenate([-w[:, half:], w[:, :half]], axis=1)


def _in_proj_weights(w_in):
    D = w_in.shape[0]
    widths = (NSA_HEADS * HEAD_DIM,) + (NSA_GROUPS * HEAD_DIM,) * 6 + (
        3 * NSA_HEADS, DIFF_HEADS * 2 * DIFF_DIM, DIFF_HEADS * 2 * DIFF_DIM, DIFF_HEADS * 2 * DIFF_DIM,
        Q_LORA, KV_LORA, MLA_ROPE, 3 * D)
    parts, o = [], 0
    for w in widths:
        parts.append(w_in[:, o:o + w])
        o += w
    nq, kc, vc, ks, vs, kw, vw, ng, dq, dk, dv, cq, ckv, kr, mg = parts
    wb = jnp.concatenate([
        _head_pad_cols(nq, NSA_HEADS, HEAD_DIM, LOG2E * HEAD_DIM ** -0.5),
        _head_pad_cols(dq, 2 * DIFF_HEADS, DIFF_DIM, LOG2E * DIFF_DIM ** -0.5),
        _head_pad_cols(dk, 2 * DIFF_HEADS, DIFF_DIM),
        _head_pad_cols(ks, NSA_GROUPS, HEAD_DIM), _head_pad_cols(vs, NSA_GROUPS, HEAD_DIM),
        _head_pad_cols(kw, NSA_GROUPS, HEAD_DIM), _head_pad_cols(vw, NSA_GROUPS, HEAD_DIM),
        dv], axis=1).astype(BF16)

    def rope_block(w):
        return jnp.pad(w, ((0, 0), (MLA_NOPE, LANES - MLA_NOPE - MLA_ROPE)))

    wf = jnp.concatenate([
        cq, kc, vc, jnp.pad(ng, ((0, 0), (0, LANES - ng.shape[1]))), ckv,
        rope_block(kr), rope_block(_rot_half_cols(kr)), mg], axis=1).astype(BF16)
    assert wb.shape[1] == ZB_WIDTH and wf.shape[1] == ZF_WIDTH
    return wb, wf


def _bf16_terms(x, n):
    terms, rest = [], np.float32(x)
    for _ in range(n):
        t = np.float32(np.asarray(rest, dtype=BF16).astype(np.float32))
        terms.append(float(t))
        rest = np.float32(rest - t)
    return terms


def _aux_const_row(slopes):
    row = np.zeros((1, ZB_WIDTH), np.float32)
    q_blocks = [(ZB_QN + h * LANES, slopes[h]) for h in range(NSA_HEADS)]
    q_blocks += [(ZB_DQ + b * LANES, slopes[NSA_HEADS + b // 2]) for b in range(2 * DIFF_HEADS)]
    for off, slope in q_blocks:
        for n, term in enumerate(_bf16_terms(slope, ALIBI_TERMS)):
            row[0, off + AUX_LANE + 2 * n] = ALIBI_RADIX * term
            row[0, off + AUX_LANE + 2 * n + 1] = term
    for off in (ZB_VS, ZB_VW):
        for g in range(NSA_GROUPS):
            row[0, off + g * LANES + AUX_LANE] = 1.0
    return jnp.asarray(row)


def _key_offset_table(tk):
    tab = np.zeros((tk, LANES), np.float32)
    c = np.arange(tk)
    for n in range(ALIBI_TERMS):
        tab[:, AUX_LANE + 2 * n] = c // ALIBI_RADIX
        tab[:, AUX_LANE + 2 * n + 1] = c % ALIBI_RADIX
    return jnp.asarray(tab, BF16)


def _rope_tables(S):
    half = MLA_ROPE // 2
    freqs = ROPE_THETA ** (-jnp.arange(half, dtype=F32) / half)
    ang = jnp.arange(S, dtype=F32)[:, None] * freqs[None, :]
    cos = jnp.concatenate([jnp.cos(ang), jnp.cos(ang)], -1)
    sin = jnp.concatenate([jnp.sin(ang), jnp.sin(ang)], -1)
    tail = jnp.zeros((S, LANES - MLA_NOPE - MLA_ROPE), F32)
    cos_q = jnp.concatenate([jnp.ones((S, MLA_NOPE), F32), cos, tail], -1)
    cos_k = jnp.concatenate([jnp.zeros((S, MLA_NOPE), F32), cos, tail], -1)
    sin_qk = jnp.concatenate([jnp.zeros((S, MLA_NOPE), F32), sin, tail], -1)
    return cos_q, sin_qk, cos_k, sin_qk


def _score_matrix_t(S):
    nch = S // CMP_STRIDE
    ratio = CMP_LEN // CMP_STRIDE
    per_sb = SEL_BLOCK // CMP_STRIDE
    sb = jnp.arange(LANES)[:, None]
    cb = jnp.arange(nch)[None, :]
    m = jnp.zeros((LANES, nch), F32)
    for jj in range(ratio):
        chunk = cb + jj
        m = m + ((chunk // per_sb == sb) & (chunk < nch)).astype(F32)
    return m.astype(BF16)


def _cmp_flat(z, B, S):
    nch = S // CMP_STRIDE
    x = z.reshape(B, S, NSA_GROUPS, HEAD_DIM).transpose(0, 2, 1, 3).reshape(B, NSA_GROUPS, nch, CMP_STRIDE * HEAD_DIM)
    nxt = jnp.roll(x, -1, axis=2)
    return jnp.concatenate([x, nxt], -1).reshape(B * NSA_GROUPS * nch, CMP_LEN * HEAD_DIM)


def _mixer_and_ffn(h, hb, layer, B, S, p, shared, alpha):
    T, D = h.shape
    tq = min(WINDOW, S)
    wide_tk = 2 * tq if S % (2 * tq) == 0 else tq
    wb, wf = _in_proj_weights(p["w_in"])
    zb = _matmul(hb, wb, BF16, min(1024, T), ZB_WIDTH // 3, "in_proj_b", const_row=shared["aux_row"])
    zf = _matmul(hb, wf, F32, min(1024, T), ZF_WIDTH // 4, "in_proj_f")
    slopes = _alibi_slopes_log2()
    nsa_slopes, diff_slopes = slopes[:NSA_HEADS], slopes[NSA_HEADS:]

    nch = S // CMP_STRIDE
    cmp_out = []
    for off, pos, w1, w2 in ((ZF_KC, p["cmp_pos_k"], p["cmp_w1_k"], p["cmp_w2_k"]),
                             (ZF_VC, p["cmp_pos_v"], p["cmp_w1_v"], p["cmp_w2_v"])):
        flat = _cmp_flat(zf[:, off:off + NSA_GROUPS * HEAD_DIM], B, S)
        w2p = jnp.pad(w2, ((0, 0), (0, LANES - HEAD_DIM))).astype(BF16)
        out = _compress_call(flat, pos.reshape(1, CMP_LEN * HEAD_DIM), w1.astype(BF16), w2p,
                             min(512, flat.shape[0]), "nsa_compress")
        cmp_out.append(out.reshape(B, NSA_GROUPS, nch, LANES))
    o_cmp, sel, any_sel = _cmp_call(B, S, zb, cmp_out[0], cmp_out[1], shared["score_t"], min(256, S))
    grp_map = [h_ // NSA_HPG for h_ in range(NSA_HEADS)]
    sel_steps = _selected_steps(any_sel, S, tq)

    def selected_sweep(steps):
        return _flash_call(B, S, zb, ZB_QN, NSA_HEADS, zb, ZB_KS, NSA_GROUPS, zb, ZB_VS, NSA_GROUPS, shared,
                           kmap=grp_map, vmap=grp_map, slopes=nsa_slopes, mode="causal", tq=tq, name="nsa_sel",
                           sel=sel, steps=steps)

    n_short = sel_steps[0].shape[1] // 2
    n_visited = jnp.max(jnp.sum(sel_steps[2] & STEP_ACTIVE, axis=1))
    o_sel = lax.cond(n_visited <= n_short,
                     lambda: selected_sweep(tuple(a[:, :n_short] for a in sel_steps)),
                     lambda: selected_sweep(sel_steps))
    o_win = _flash_call(B, S, zb, ZB_QN, NSA_HEADS, zb, ZB_KW, NSA_GROUPS, zb, ZB_VW, NSA_GROUPS, shared,
                        kmap=grp_map, vmap=grp_map, slopes=nsa_slopes, mode="window", tq=tq, name="nsa_win")

    lam_init = 0.8 - 0.6 * math.exp(-0.3 * layer)
    n_maps = 2 * DIFF_HEADS
    o_diff = _flash_call(B, S, zb, ZB_DQ, n_maps, zb, ZB_DK, n_maps, zb, ZB_DV, DIFF_HEADS, shared,
                         kmap=list(range(n_maps)), vmap=[m_ // 2 for m_ in range(n_maps)],
                         slopes=[diff_slopes[m_ // 2] for m_ in range(n_maps)],
                         mode="causal", tq=tq, tk=wide_tk, name="diff_attn", lam=p["diff_lambda"],
                         subg=p["diff_subln_g"].reshape(1, 2 * DIFF_DIM), lam_init=lam_init)

    w_uq = p["mla_w_uq"].reshape(Q_LORA, MLA_HEADS, MLA_NOPE + MLA_ROPE)
    wq = jnp.pad(w_uq, ((0, 0), (0, 0), (0, LANES - MLA_NOPE - MLA_ROPE))).reshape(Q_LORA, MLA_HEADS * LANES)
    rot = jnp.stack([_rot_half_cols(w_uq[:, h_, MLA_NOPE:]) for h_ in range(MLA_HEADS)], axis=1)
    wqr = jnp.pad(rot, ((0, 0), (0, 0), (MLA_NOPE, LANES - MLA_NOPE - MLA_ROPE))).reshape(Q_LORA, MLA_HEADS * LANES)
    w_ukv = p["mla_w_ukv"].reshape(KV_LORA, MLA_HEADS, MLA_NOPE + MLA_V)
    wk = _head_pad_cols(w_ukv[:, :, :MLA_NOPE].reshape(KV_LORA, -1), MLA_HEADS, MLA_NOPE)
    wv = _head_pad_cols(w_ukv[:, :, MLA_NOPE:].reshape(KV_LORA, -1), MLA_HEADS, MLA_V)
    qm, km, vm = _mla_prep_call(zf, shared["rope"], p["mla_q_norm_g"].reshape(1, Q_LORA),
                                p["mla_kv_norm_g"].reshape(1, KV_LORA), wq.astype(BF16), wqr.astype(BF16),
                                wk.astype(BF16), wv.astype(BF16), S, min(512, S))
    ident = list(range(MLA_HEADS))
    o_mla = _flash_call(B, S, qm, 0, MLA_HEADS, km, 0, MLA_HEADS, vm, 0, MLA_HEADS, shared, kmap=ident, vmap=ident,
                        slopes=[0.0] * MLA_HEADS, mode="causal", tq=tq, tk=wide_tk, name="mla_attn")

    h1, h1b = _combine_call(
        o_cmp, o_sel, o_win, zf, o_diff, o_mla, h, shared["gate_expand"],
        _head_pad_rows(p["w_br_nsa"], NSA_HEADS, HEAD_DIM).astype(BF16), p["w_br_diff"].astype(BF16),
        _head_pad_rows(p["w_br_mla"], MLA_HEADS, MLA_V).astype(BF16), p["w_out"].astype(BF16),
        p["ln1_g"].reshape(1, D), p["ln1_b"].reshape(1, D), alpha, min(256, T))

    return _moe_call(h1, h1b, shared["router_wt"], shared["router_b"], shared["moe_w1"],
                     shared["moe_w3"], shared["moe_w2"], layer, p["ln2_g"].reshape(1, D),
                     p["ln2_b"].reshape(1, D), alpha, min(512, T))


def kernel(x, ln_in_g, ln_in_b, w_in, cmp_pos_k, cmp_w1_k, cmp_w2_k, cmp_pos_v, cmp_w1_v, cmp_w2_v, diff_lambda, diff_subln_g, mla_q_norm_g, mla_kv_norm_g, mla_w_uq, mla_w_ukv, w_br_nsa, w_br_diff, w_br_mla, w_out, ln1_g, ln1_b, router_w, router_b, moe_w1, moe_w3, moe_w2, ln2_g, ln2_b):
    B, S, D = x.shape
    depth = w_in.shape[0]
    alpha = (2 * depth) ** 0.25
    T = B * S
    tq = min(WINDOW, S)
    idx = jnp.arange(tq, dtype=jnp.int32)
    gate_rows = jnp.arange(LANES)[:, None]
    gate_cols = jnp.arange(NSA_HEADS * LANES)[None, :] // LANES
    shared = {
        "rel": (idx[None, :] - idx[:, None]).astype(F32),
        "aux_row": _aux_const_row(_alibi_slopes_log2()),
        "block_onehot": ((jnp.arange(S)[:, None] // SEL_BLOCK) == jnp.arange(LANES)[None, :]).astype(BF16),
        "score_t": _score_matrix_t(S),
        "rope": _rope_tables(S),
        "gate_expand": jnp.stack([(gate_rows == gate_cols * 3 + j) for j in range(3)]).astype(BF16),
        "router_wt": router_w.T.astype(BF16),
        "router_b": router_b.reshape(N_EXPERTS, 1).astype(F32),
        "moe_w1": moe_w1.astype(BF16).reshape((depth * N_EXPERTS,) + moe_w1.shape[2:]),
        "moe_w3": moe_w3.astype(BF16).reshape((depth * N_EXPERTS,) + moe_w3.shape[2:]),
        "moe_w2": moe_w2.astype(BF16).reshape((depth * N_EXPERTS,) + moe_w2.shape[2:]),
    }
    per_layer = dict(w_in=w_in, cmp_pos_k=cmp_pos_k, cmp_w1_k=cmp_w1_k, cmp_w2_k=cmp_w2_k, cmp_pos_v=cmp_pos_v,
                     cmp_w1_v=cmp_w1_v, cmp_w2_v=cmp_w2_v, diff_lambda=diff_lambda, diff_subln_g=diff_subln_g,
                     mla_q_norm_g=mla_q_norm_g, mla_kv_norm_g=mla_kv_norm_g, mla_w_uq=mla_w_uq, mla_w_ukv=mla_w_ukv,
                     w_br_nsa=w_br_nsa, w_br_diff=w_br_diff, w_br_mla=w_br_mla, w_out=w_out, ln1_g=ln1_g,
                     ln1_b=ln1_b, ln2_g=ln2_g, ln2_b=ln2_b)
    h, hb = _ln_call(x.reshape(T, D), ln_in_g, ln_in_b, min(512, T))
    for l in range(depth):
        p = {k: v[l] for k, v in per_layer.items()}
        h, hb = _mixer_and_ffn(h, hb, l, B, S, p, shared, alpha)
    return h.reshape(B, S, D)
```

```python
import functools
import math

import numpy as np
import jax
import jax.numpy as jnp
from jax import lax
from jax.experimental import pallas as pl
from jax.experimental.pallas import tpu as pltpu

F32 = jnp.float32
BF16 = jnp.bfloat16

LANES = 128
HEAD_DIM = 64
NSA_HEADS = 8
NSA_GROUPS = 2
NSA_HPG = NSA_HEADS // NSA_GROUPS
CMP_LEN = 32
CMP_STRIDE = 16
CMP_HIDDEN = 256
SEL_BLOCK = 64
N_SELECT = 16
WINDOW = 512
FORCE_BONUS = 1.0e4
DIFF_HEADS = 4
DIFF_DIM = 64
MLA_HEADS = 8
MLA_NOPE = 64
MLA_ROPE = 32
MLA_V = 64
Q_LORA = 256
KV_LORA = 128
ROPE_THETA = 10000.0
N_EXPERTS = 16
N_GROUPS = 4
EXPERTS_PER_GROUP = N_EXPERTS // N_GROUPS
D_FF_EXPERT = 512
LN_EPS = 1e-5
RMS_EPS = 1e-6
NEG = -1e30
LOG2E = math.log2(math.e)
MASK_BIG = 2.0 ** 100
AUX_LANE = HEAD_DIM
ALIBI_TERMS = 3
ALIBI_RADIX = 16
QK_LOOKAHEAD = 2
STEP_ACTIVE, STEP_FIRST, STEP_DIAG, STEP_GROUP0, STEP_HIGH = 1, 2, 4, 8, 32

VMEM_LIMIT = 56 * 1024 * 1024

ZB_QN, ZB_DQ, ZB_DK, ZB_KS, ZB_VS, ZB_KW, ZB_VW, ZB_DV, ZB_WIDTH = 0, 1024, 2048, 3072, 3328, 3584, 3840, 4096, 4608
ZF_CQ, ZF_KC, ZF_VC, ZF_NG, ZF_CKV, ZF_KR, ZF_KRR, ZF_MG, ZF_WIDTH = 0, 256, 384, 512, 640, 768, 896, 1024, 4096


def _alibi_slopes_log2():
    n = NSA_HEADS + DIFF_HEADS
    return [LOG2E * 2.0 ** (-8.0 * i / n) for i in range(1, n + 1)]


def _params(sem):
    return pltpu.CompilerParams(dimension_semantics=sem, vmem_limit_bytes=VMEM_LIMIT)


def _layer_norm(x, g, b):
    mu = jnp.mean(x, -1, keepdims=True)
    xc = x - mu
    var = jnp.mean(xc * xc, -1, keepdims=True)
    return xc * lax.rsqrt(var + LN_EPS) * g + b


def _rms_norm(x, g):
    return x * lax.rsqrt(jnp.mean(x * x, -1, keepdims=True) + RMS_EPS) * g


def _dot(a, b):
    return jnp.dot(a, b, preferred_element_type=F32)


def _dot_nt(a, b):
    return lax.dot_general(a, b, (((1,), (1,)), ((), ())), preferred_element_type=F32)


def _ln_kernel(x_ref, g_ref, b_ref, of_ref, ob_ref):
    y = _layer_norm(x_ref[...], g_ref[...], b_ref[...])
    of_ref[...] = y
    ob_ref[...] = y.astype(BF16)


def _ln_call(x, g, b, tm):
    T, D = x.shape
    row = pl.BlockSpec((tm, D), lambda i: (i, 0))
    vec = pl.BlockSpec((1, D), lambda i: (0, 0))
    return pl.pallas_call(
        _ln_kernel, grid=(T // tm,), in_specs=[row, vec, vec], out_specs=[row, row],
        out_shape=[jax.ShapeDtypeStruct((T, D), F32), jax.ShapeDtypeStruct((T, D), BF16)],
        compiler_params=_params(("parallel",)), name="ln_in")(x, g.reshape(1, D), b.reshape(1, D))


def _mm_kernel(a_ref, w_ref, o_ref):
    o_ref[...] = _dot(a_ref[...], w_ref[...]).astype(o_ref.dtype)


def _mm_const_kernel(a_ref, w_ref, c_ref, o_ref):
    o_ref[...] = (_dot(a_ref[...], w_ref[...]) + c_ref[...]).astype(o_ref.dtype)


def _matmul(a, w, out_dtype, tm, tn, name, const_row=None):
    M, K = a.shape
    N = w.shape[1]
    in_specs = [pl.BlockSpec((tm, K), lambda i, j: (i, 0)), pl.BlockSpec((K, tn), lambda i, j: (0, j))]
    args = [a, w]
    if const_row is not None:
        in_specs.append(pl.BlockSpec((1, tn), lambda i, j: (0, j)))
        args.append(const_row)
    return pl.pallas_call(
        _mm_kernel if const_row is None else _mm_const_kernel, grid=(M // tm, N // tn), in_specs=in_specs,
        out_specs=pl.BlockSpec((tm, tn), lambda i, j: (i, j)),
        out_shape=jax.ShapeDtypeStruct((M, N), out_dtype),
        compiler_params=_params(("parallel", "arbitrary")), name=name)(*args)


def _compress_kernel(x_ref, pos_ref, w1_ref, w2_ref, o_ref):
    flat = (x_ref[...] + pos_ref[...]).astype(BF16)
    hid = jax.nn.gelu(_dot(flat, w1_ref[...]))
    o_ref[...] = _dot(hid.astype(BF16), w2_ref[...]).astype(o_ref.dtype)


def _compress_call(flat, pos_flat, w1, w2p, tm, name):
    R, W = flat.shape
    return pl.pallas_call(
        _compress_kernel, grid=(R // tm,),
        in_specs=[pl.BlockSpec((tm, W), lambda i: (i, 0)), pl.BlockSpec((1, W), lambda i: (0, 0)),
                  pl.BlockSpec(w1.shape, lambda i: (0, 0)), pl.BlockSpec(w2p.shape, lambda i: (0, 0))],
        out_specs=pl.BlockSpec((tm, LANES), lambda i: (i, 0)),
        out_shape=jax.ShapeDtypeStruct((R, LANES), BF16),
        compiler_params=_params(("parallel",)), name=name)(flat, pos_flat, w1, w2p)


def _mla_prep_kernel(cq_ref, ckv_ref, kr_ref, krr_ref, cq128_ref, sq128_ref, ck128_ref, sk128_ref,
                     qg_ref, kvg_ref, wq_ref, wqr_ref, wk_ref, wv_ref, q_out, k_out, v_out, *, scale):
    cqn = _rms_norm(cq_ref[...], qg_ref[...]).astype(BF16)
    q_main = _dot(cqn, wq_ref[...])
    q_rot = _dot(cqn, wqr_ref[...])
    ckvn = _rms_norm(ckv_ref[...], kvg_ref[...]).astype(BF16)
    k_nope = _dot(ckvn, wk_ref[...])
    lane = lax.broadcasted_iota(jnp.int32, (1, MLA_HEADS * LANES), 1)
    sum_lane = ((lane & (LANES - 1)) == AUX_LANE).astype(F32)
    v_out[...] = (_dot(ckvn, wv_ref[...]) + sum_lane).astype(BF16)
    k_rope = kr_ref[...] * ck128_ref[...] + krr_ref[...] * sk128_ref[...]
    cq128 = cq128_ref[...]
    sq128 = sq128_ref[...]
    for h in range(MLA_HEADS):
        sl = slice(h * LANES, (h + 1) * LANES)
        q_out[:, sl] = ((q_main[:, sl] * cq128 + q_rot[:, sl] * sq128) * scale).astype(BF16)
        k_out[:, sl] = (k_nope[:, sl] + k_rope).astype(BF16)


def _mla_prep_call(zf, tabs, qg, kvg, wq, wqr, wk, wv, S, tm):
    T = zf.shape[0]
    npos = S // tm
    HW = MLA_HEADS * LANES

    def col(width, off):
        return pl.BlockSpec((tm, width), lambda i: (i, off // width))

    tab = pl.BlockSpec((tm, LANES), lambda i: (i % npos, 0))

    def full(a):
        return pl.BlockSpec(a.shape, lambda i: (0, 0))

    out = pl.BlockSpec((tm, HW), lambda i: (i, 0))
    return pl.pallas_call(
        functools.partial(_mla_prep_kernel, scale=LOG2E * (MLA_NOPE + MLA_ROPE) ** -0.5), grid=(T // tm,),
        in_specs=[col(Q_LORA, ZF_CQ), col(KV_LORA, ZF_CKV), col(LANES, ZF_KR), col(LANES, ZF_KRR),
                  tab, tab, tab, tab, full(qg), full(kvg), full(wq), full(wqr), full(wk), full(wv)],
        out_specs=[out, out, out],
        out_shape=[jax.ShapeDtypeStruct((T, HW), BF16)] * 3,
        compiler_params=_params(("parallel",)), name="mla_prep")(
            zf, zf, zf, zf, *tabs, qg, kvg, wq, wqr, wk, wv)


def _cmp_kernel(q_ref, kc_ref, vc_ref, kctab_ref, mt_ref, o_ref, sel_ref, any_ref, sc_ref, *, tq, nch, nsb, nsel):
    i = pl.program_id(1)
    t0 = i * tq
    row_valid = t0 + lax.broadcasted_iota(jnp.int32, (tq, 1), 0) >= CMP_LEN - 1

    def attend(ncols):
        r = lax.broadcasted_iota(jnp.int32, (tq, ncols), 0)
        c = lax.broadcasted_iota(jnp.int32, (tq, ncols), 1)
        valid = (t0 + r >= CMP_STRIDE * c + (CMP_LEN - 1)) & (c < nch - 1)
        mt = mt_ref[:, :ncols]
        for g in range(NSA_GROUPS):
            kc = kc_ref[0, g, :ncols, :] + kctab_ref[:ncols, :]
            vc = vc_ref[0, g, :ncols, :]
            p_grp = jnp.zeros((tq, ncols), F32)
            for hh in range(NSA_HPG):
                h = g * NSA_HPG + hh
                sl = slice(h * LANES, (h + 1) * LANES)
                z = jnp.where(valid, _dot_nt(q_ref[:, sl], kc), NEG)
                e = jnp.exp2(z - jnp.max(z, -1, keepdims=True))
                inv = jnp.where(row_valid, 1.0 / jnp.maximum(jnp.sum(e, -1, keepdims=True), 1e-30), 0.0)
                p = e * inv
                o_ref[:, sl] = _dot(p.astype(BF16), vc).astype(BF16)
                p_grp = p_grp + p
            p1 = p_grp.astype(BF16)
            r1 = p_grp - p1.astype(F32)
            p2 = r1.astype(BF16)
            p3 = (r1 - p2.astype(F32)).astype(BF16)
            sc_ref[g] = _dot_nt(mt, p1) + _dot_nt(mt, p2) + _dot_nt(mt, p3)

    widths = sorted({min(nch, w) for w in range(LANES, nch + LANES, LANES)})
    ended = (t0 + tq) // CMP_STRIDE - (CMP_LEN // CMP_STRIDE - 1)
    for n, width in enumerate(widths):
        lo = widths[n - 1] if n else -1
        hi = width if n + 1 < len(widths) else nch + LANES
        @pl.when((ended > lo) & (ended <= hi))
        def _variant():
            attend(width)

    sb = lax.broadcasted_iota(jnp.int32, (LANES, tq), 0)
    sb_f = sb.astype(F32)
    t = t0 + lax.broadcasted_iota(jnp.int32, (LANES, tq), 1)
    cur = lax.shift_right_arithmetic(t, SEL_BLOCK.bit_length() - 1)
    forced = (sb == 0) | (sb == cur) | (sb == cur - 1)
    started = sb * SEL_BLOCK <= t
    for g in range(NSA_GROUPS):
        sc = sc_ref[g]
        sc = jnp.where(forced, sc + FORCE_BONUS, sc)
        sc = jnp.where(started, sc, -FORCE_BONUS)
        sc = jnp.where(sb < nsb, sc, -jnp.inf)
        sel_t = jnp.zeros((LANES, tq), F32)
        for _ in range(nsel):
            m = jnp.max(sc, axis=0, keepdims=True)
            cand = jnp.where(sc == m, sb_f, float(LANES))
            idx = jnp.min(cand, axis=0, keepdims=True)
            hit = sb_f == idx
            sel_t = jnp.where(hit, 1.0, sel_t)
            sc = jnp.where(hit, -jnp.inf, sc)
        sel_mat = sel_t.T
        sel_ref[0, g] = ((sel_mat - 1.0) * MASK_BIG).astype(BF16)
        any_ref[0, 0, g:g + 1, :] = jnp.max(sel_mat, axis=0, keepdims=True)


def _cmp_key_table(nch):
    tab = np.zeros((nch, LANES), np.float32)
    c = np.arange(nch)
    for n in range(ALIBI_TERMS):
        tab[:, AUX_LANE + 2 * n] = CMP_STRIDE * (c // ALIBI_RADIX)
        tab[:, AUX_LANE + 2 * n + 1] = CMP_STRIDE * (c % ALIBI_RADIX)
    return jnp.asarray(tab, BF16)


def _cmp_call(B, S, zb, kcmp, vcmp, mt, tq):
    nch = S // CMP_STRIDE
    nsb = S // SEL_BLOCK
    nq = S // tq
    HW = NSA_HEADS * LANES
    kern = functools.partial(_cmp_kernel, tq=tq, nch=nch, nsb=nsb, nsel=min(N_SELECT, nsb))
    cmp_spec = pl.BlockSpec((1, NSA_GROUPS, nch, LANES), lambda b, i: (b, 0, 0, 0))
    return pl.pallas_call(
        kern, grid=(B, nq),
        in_specs=[pl.BlockSpec((tq, HW), lambda b, i: (b * nq + i, ZB_QN // HW)), cmp_spec, cmp_spec,
                  pl.BlockSpec((nch, LANES), lambda b, i: (0, 0)), pl.BlockSpec(mt.shape, lambda b, i: (0, 0))],
        out_specs=[pl.BlockSpec((tq, HW), lambda b, i: (b * nq + i, 0)),
                   pl.BlockSpec((1, NSA_GROUPS, tq, LANES), lambda b, i: (b, 0, i, 0)),
                   pl.BlockSpec((1, 1, NSA_GROUPS, LANES), lambda b, i: (b, i, 0, 0))],
        out_shape=[jax.ShapeDtypeStruct((B * S, HW), BF16),
                   jax.ShapeDtypeStruct((B, NSA_GROUPS, S, LANES), BF16),
                   jax.ShapeDtypeStruct((B, nq, NSA_GROUPS, LANES), F32)],
        scratch_shapes=[pltpu.VMEM((NSA_GROUPS, LANES, tq), F32)],
        compiler_params=_params(("parallel", "parallel")), name="nsa_cmp")(zb, kcmp, vcmp, _cmp_key_table(nch), mt)


def _flash_kernel(*refs, n_heads, kmap, vmap, slopes, mode, has_sel, sum_lane, finalize, tq, tk, lam_init):
    kj_ref, fl_ref = refs[1], refs[2]
    it = iter(refs[3:])
    q_ref, k_ref, v_ref, rel_ref, ktab_ref = next(it), next(it), next(it), next(it), next(it)
    if has_sel:
        sel_ref, et_ref = next(it), next(it)
    if finalize == "diff":
        lam_ref, subg_ref = next(it), next(it)
    o_ref, m_ref, acc_ref = next(it), next(it), next(it)
    l_ref = None if sum_lane else next(it)

    b = pl.program_id(0)
    s = pl.program_id(1)
    kv = kj_ref[b, s]
    flags = fl_ref[b, s]
    is_diag = (flags & STEP_DIAG) != 0
    is_off = ((flags & STEP_ACTIVE) != 0) & jnp.logical_not(is_diag)

    @pl.when((flags & STEP_FIRST) != 0)
    def _init():
        m_ref[...] = jnp.full(m_ref.shape, NEG, F32)
        acc_ref[...] = jnp.zeros(acc_ref.shape, F32)
        if l_ref is not None:
            l_ref[...] = jnp.zeros(l_ref.shape, F32)

    def run_heads(heads, variant):
        rows = tq if variant == "diag_lo" else tk
        if variant in ("diag", "diag_lo"):
            mask = rel_ref[...] <= 0
        elif variant == "diag_hi":
            mask = jnp.concatenate([rel_ref[...] - tq, rel_ref[...]], axis=1) <= 0
        elif variant == "prev":
            mask = rel_ref[...] > 0
        else:
            mask = None
        key0 = (kv * tk).astype(F32)
        k_blocks = {}

        def scores(h):
            kb = kmap[h]
            if kb not in k_blocks:
                k = k_ref[:rows, kb * LANES:(kb + 1) * LANES]
                if slopes[h] != 0.0:
                    k = k + ktab_ref[:rows, :]
                if has_sel:
                    k = jnp.concatenate([k, et_ref[...]], axis=1)
                k_blocks[kb] = k
            q = q_ref[:, h * LANES:(h + 1) * LANES]
            if has_sel:
                q = jnp.concatenate([q, sel_ref[0, h // NSA_HPG]], axis=1)
            return _dot_nt(q, k_blocks[kb])

        pending = [scores(h) for h in heads[:QK_LOOKAHEAD]]
        for n, h in enumerate(heads):
            u = pending.pop(0)
            if n + QK_LOOKAHEAD < len(heads):
                pending.append(scores(heads[n + QK_LOOKAHEAD]))
            v = v_ref[:rows, vmap[h] * LANES:(vmap[h] + 1) * LANES]
            delta = slopes[h] * key0 if slopes[h] != 0.0 else 0.0
            if mask is not None:
                u = jnp.where(mask, u, NEG)
            m_prev = m_ref[h]
            m_new = jnp.maximum(m_prev, jnp.max(u, -1, keepdims=True) + delta)
            alpha = jnp.exp2(m_prev - m_new)
            shift = m_new - delta
            psum = None
            chunks = []
            for c in range(rows // LANES):
                pc = jnp.exp2(u[:, c * LANES:(c + 1) * LANES] - shift)
                if l_ref is not None:
                    psum = pc if psum is None else psum + pc
                chunks.append(pc.astype(BF16))
            p = jnp.concatenate(chunks, axis=1)
            if l_ref is not None:
                l_ref[h] = alpha * l_ref[h] + psum
            acc_ref[h] = alpha * acc_ref[h] + _dot(p, v)
            m_ref[h] = m_new

    def normalised(h):
        acc = acc_ref[h]
        if l_ref is None:
            l = acc[:, AUX_LANE:AUX_LANE + 1]
        else:
            l = jnp.sum(l_ref[h], -1, keepdims=True)
        return acc * (1.0 / jnp.maximum(l, 1e-30))

    all_heads = list(range(n_heads))

    @pl.when(is_off)
    def _off():
        if has_sel:
            for g in range(NSA_GROUPS):
                @pl.when((flags & (STEP_GROUP0 << g)) != 0)
                def _group():
                    run_heads(all_heads[g * NSA_HPG:(g + 1) * NSA_HPG], "off")
        else:
            run_heads(all_heads, "prev" if mode == "window" else "off")

    if tk == 2 * tq:
        @pl.when(is_diag & ((flags & STEP_HIGH) == 0))
        def _diag_lo():
            run_heads(all_heads, "diag_lo")

        @pl.when(is_diag & ((flags & STEP_HIGH) != 0))
        def _diag_hi():
            run_heads(all_heads, "diag_hi")

    @pl.when(is_diag)
    def _diag():
        if tk == tq:
            run_heads(all_heads, "diag")
        if finalize == "plain":
            head_lanes = lax.broadcasted_iota(jnp.int32, (tq, LANES), 1) < HEAD_DIM
            for h in range(n_heads):
                o_ref[:, h * LANES:(h + 1) * LANES] = jnp.where(head_lanes, normalised(h), 0.0).astype(o_ref.dtype)
        else:
            lp = lam_ref[...]
            lam = (jnp.exp(jnp.sum(lp[0:1] * lp[1:2], -1, keepdims=True))
                   - jnp.exp(jnp.sum(lp[2:3] * lp[3:4], -1, keepdims=True)) + lam_init)
            for hd in range(n_heads // 2):
                o = normalised(2 * hd) - lam * normalised(2 * hd + 1)
                o = _rms_norm(o, subg_ref[...]) * (1.0 - lam_init)
                o_ref[:, hd * LANES:(hd + 1) * LANES] = o.astype(o_ref.dtype)


def _static_steps(B, nq, mode, key_ratio=1):
    qi, kj, fl = [], [], []
    for i in range(nq):
        first_j = 0 if mode == "causal" else max(i - 1, 0)
        last_j = i // key_ratio
        for j in range(first_j, last_j + 1):
            qi.append(i)
            kj.append(j)
            fl.append(STEP_ACTIVE | (STEP_FIRST if j == first_j else 0) | (STEP_DIAG if j == last_j else 0)
                      | (STEP_HIGH if j == last_j and i % key_ratio == 1 else 0))
    tile = lambda a: jnp.tile(jnp.asarray(a, jnp.int32)[None], (B, 1))
    return tile(qi), tile(kj), tile(fl)


def _selected_steps(any_sel, S, tq):
    B, nqc, G, _ = any_sel.shape
    nq = nk = S // tq
    per_tile = tq // SEL_BLOCK
    a = any_sel.reshape(B, nq, nqc // nq, G, LANES).max(axis=2) > 0
    a = a[..., :nk * per_tile].reshape(B, nq, G, nk, per_tile).any(-1)
    ii = jnp.arange(nq)[:, None]
    jj = jnp.arange(nk)[None, :]
    g_act = (a & (jj <= ii)[None, :, None, :]) | (ii == jj)[None, :, None, :]
    act = g_act.any(2)
    n_steps = nq * (nq + 1) // 2
    key = jnp.where(act, (ii * nk + jj)[None], nq * nk).reshape(B, nq * nk)
    order = jnp.sort(key, axis=1)[:, :n_steps]
    valid = order < nq * nk
    order = jnp.where(valid, order, nq * nk - 1)
    qi, kj = order // nk, order % nk
    first = valid & (qi != jnp.concatenate([jnp.full((B, 1), -1, qi.dtype), qi[:, :-1]], axis=1))
    g_bits = jnp.take_along_axis(g_act.transpose(0, 2, 1, 3).reshape(B, G, nq * nk), order[:, None, :], axis=2)
    flags = valid * STEP_ACTIVE + first * STEP_FIRST + (valid & (qi == kj)) * STEP_DIAG
    for g in range(G):
        flags = flags + (valid & g_bits[:, g]) * (STEP_GROUP0 << g)
    return qi.astype(jnp.int32), kj.astype(jnp.int32), flags.astype(jnp.int32)


def _flash_call(B, S, q_arr, q_off, n_heads, k_arr, k_off, k_blocks, v_arr, v_off, v_blocks, shared, *, kmap, vmap,
                slopes, mode, tq, name, tk=None, steps=None, sel=None, lam=None, subg=None, lam_init=0.0):
    tk = tq if tk is None else tk
    nq, nk = S // tq, S // tk
    QW, KW, VW = n_heads * LANES, k_blocks * LANES, v_blocks * LANES
    if mode == "window":
        assert tq == WINDOW
    assert tk == tq or (tk == 2 * tq and mode == "causal" and sel is None)
    if steps is None:
        steps = _static_steps(B, nq, mode, tk // tq)
    n_steps = steps[0].shape[1]
    finalize = "diff" if lam is not None else "plain"
    sum_lane = finalize == "plain"
    out_heads = n_heads // 2 if finalize == "diff" else n_heads
    in_specs = [pl.BlockSpec((tq, QW), lambda b, s, qi, kj, fl: (b * nq + qi[b, s], q_off // QW)),
                pl.BlockSpec((tk, KW), lambda b, s, qi, kj, fl: (b * nk + kj[b, s], k_off // KW)),
                pl.BlockSpec((tk, VW), lambda b, s, qi, kj, fl: (b * nk + kj[b, s], v_off // VW)),
                pl.BlockSpec((tq, tq), lambda b, s, qi, kj, fl: (0, 0)),
                pl.BlockSpec((tk, LANES), lambda b, s, qi, kj, fl: (0, 0))]
    args = [q_arr, k_arr, v_arr, shared["rel"], _key_offset_table(tk)]
    if sel is not None:
        in_specs += [pl.BlockSpec((1, NSA_GROUPS, tq, LANES), lambda b, s, qi, kj, fl: (b, 0, qi[b, s], 0)),
                     pl.BlockSpec((tk, LANES), lambda b, s, qi, kj, fl: (kj[b, s], 0))]
        args += [sel, shared["block_onehot"]]
    if finalize == "diff":
        in_specs += [pl.BlockSpec(lam.shape, lambda b, s, qi, kj, fl: (0, 0)),
                     pl.BlockSpec(subg.shape, lambda b, s, qi, kj, fl: (0, 0))]
        args += [lam, subg]
    kern = functools.partial(_flash_kernel, n_heads=n_heads, kmap=kmap, vmap=vmap, slopes=slopes, mode=mode,
                             has_sel=sel is not None, sum_lane=sum_lane, finalize=finalize, tq=tq, tk=tk,
                             lam_init=lam_init)
    stat = pltpu.VMEM((n_heads, tq, LANES), F32)
    grid_spec = pltpu.PrefetchScalarGridSpec(
        num_scalar_prefetch=3, grid=(B, n_steps), in_specs=in_specs,
        out_specs=pl.BlockSpec((tq, out_heads * LANES), lambda b, s, qi, kj, fl: (b * nq + qi[b, s], 0)),
        scratch_shapes=[stat, stat] if sum_lane else [stat, stat, stat])
    return pl.pallas_call(
        kern, grid_spec=grid_spec, out_shape=jax.ShapeDtypeStruct((B * S, out_heads * LANES), BF16),
        compiler_params=_params(("parallel", "arbitrary")), name=name)(*steps, *args)


def _combine_kernel(oc_ref, os_ref, ow_ref, ng_ref, od_ref, om_ref, mg0_ref, mg1_ref, mg2_ref, h_ref,
                    eg_ref, wn_ref, wd_ref, wm_ref, wo_ref, g_ref, b_ref, of_ref, ob_ref, *, alpha):
    sg = jax.nn.sigmoid(ng_ref[...]).astype(BF16)
    o_nsa = (_dot(sg, eg_ref[0]) * oc_ref[...].astype(F32)
             + _dot(sg, eg_ref[1]) * os_ref[...].astype(F32)
             + _dot(sg, eg_ref[2]) * ow_ref[...].astype(F32))
    y = (jax.nn.sigmoid(mg0_ref[...]) * _dot(o_nsa.astype(BF16), wn_ref[...])
         + jax.nn.sigmoid(mg1_ref[...]) * _dot(od_ref[...], wd_ref[...])
         + jax.nn.sigmoid(mg2_ref[...]) * _dot(om_ref[...], wm_ref[...]))
    mix = _dot(y.astype(BF16), wo_ref[...])
    hn = _layer_norm(alpha * h_ref[...] + mix, g_ref[...], b_ref[...])
    of_ref[...] = hn
    ob_ref[...] = hn.astype(BF16)


def _combine_call(oc, os_, ow, zf, od, om, h, eg, wn, wd, wm, wo, g, b, alpha, tm):
    T, D = h.shape

    def row(width, blk=0):
        return pl.BlockSpec((tm, width), lambda i: (i, blk))

    def full(a):
        nd = a.ndim
        return pl.BlockSpec(a.shape, lambda i: (0,) * nd)

    return pl.pallas_call(
        functools.partial(_combine_kernel, alpha=alpha), grid=(T // tm,),
        in_specs=[row(oc.shape[1]), row(os_.shape[1]), row(ow.shape[1]), row(LANES, ZF_NG // LANES),
                  row(od.shape[1]), row(om.shape[1]),
                  row(D, ZF_MG // D), row(D, ZF_MG // D + 1), row(D, ZF_MG // D + 2), row(D),
                  full(eg), full(wn), full(wd), full(wm), full(wo), full(g), full(b)],
        out_specs=[row(D), row(D)],
        out_shape=[jax.ShapeDtypeStruct((T, D), F32), jax.ShapeDtypeStruct((T, D), BF16)],
        compiler_params=_params(("parallel",)), name="mixer_combine")(
            oc, os_, ow, zf, od, om, zf, zf, zf, h, eg, wn, wd, wm, wo, g, b)


def _route(logits_t, rb):
    aff = jax.nn.sigmoid(logits_t)
    selv = aff + rb
    a_rows = [aff[e:e + 1] for e in range(N_EXPERTS)]
    s_rows = [selv[e:e + 1] for e in range(N_EXPERTS)]
    npg = EXPERTS_PER_GROUP
    best, grp = None, None
    for g in range(N_GROUPS):
        v = s_rows[g * npg:(g + 1) * npg]
        top2 = None
        for a in range(npg):
            for b in range(a + 1, npg):
                pair = v[a] + v[b]
                top2 = pair if top2 is None else jnp.maximum(top2, pair)
        if g == 0:
            best, grp = top2, jnp.zeros_like(top2, dtype=jnp.int32)
        else:
            better = top2 > best
            grp = jnp.where(better, g, grp)
            best = jnp.where(better, top2, best)

    def pick(rows, k):
        out = rows[k]
        for g in range(1, N_GROUPS):
            out = jnp.where(grp == g, rows[g * npg + k], out)
        return out

    v = [pick(s_rows, k) for k in range(npg)]
    a = [pick(a_rows, k) for k in range(npg)]
    b1, i1 = v[0], jnp.zeros_like(grp)
    for k in range(1, npg):
        gt = v[k] > b1
        i1 = jnp.where(gt, k, i1)
        b1 = jnp.where(gt, v[k], b1)
    b2, i2 = jnp.full_like(b1, -jnp.inf), jnp.zeros_like(grp)
    for k in range(npg):
        ok = (i1 != k) & (v[k] > b2)
        i2 = jnp.where(ok, k, i2)
        b2 = jnp.where(ok, v[k], b2)
    g1 = sum(jnp.where(i1 == k, a[k], 0.0) for k in range(npg))
    g2 = sum(jnp.where(i2 == k, a[k], 0.0) for k in range(npg))
    den = g1 + g2
    w1, w2 = g1 / den, g2 / den
    sub = lax.broadcasted_iota(jnp.int32, (LANES, logits_t.shape[1]), 0)
    comb = jnp.zeros(sub.shape, F32)
    for e in range(N_EXPERTS):
        g, k = divmod(e, npg)
        in_g = grp == g
        row = jnp.where(in_g & (i1 == k), w1, 0.0) + jnp.where(in_g & (i2 == k), w2, 0.0)
        comb = jnp.where(sub == e, row, comb)
    return comb, grp


GRP_LANE, POS_LANE = N_EXPERTS, N_EXPERTS + 1
MOE_CHUNK = 160


def _moe_kernel(x_ref, xb_ref, rwt_ref, rb_ref, tri_ref, w1_ref, w3_ref, w2_ref, g_ref, b_ref, of_ref, ob_ref,
                tok_ref, tok3_ref, rowv_ref, acc_ref, cnt_ref, *, alpha, tm):
    g = pl.program_id(1)

    @pl.when(g == 0)
    def _routing():
        comb_t, grp = _route(_dot_nt(rwt_ref[...], xb_ref[...]), rb_ref[...])
        sub16 = lax.broadcasted_iota(jnp.int32, (16, tm), 0)
        onehot = sub16 == grp
        ranks = _dot(onehot.astype(BF16), tri_ref[...])
        pos = jnp.sum(jnp.where(onehot, ranks, 0.0), axis=0, keepdims=True)
        grp_f = grp.astype(F32)
        for gg in range(N_GROUPS):
            cnt_ref[gg] = jnp.sum((grp == gg).astype(jnp.int32))
        rowv_ref[0:1, :] = grp_f
        rowv_ref[1:2, :] = pos
        sub = lax.broadcasted_iota(jnp.int32, (LANES, tm), 0)
        tok = jnp.where(sub == GRP_LANE, grp_f, jnp.where(sub == POS_LANE, pos, comb_t)).T
        tok_ref[...] = tok
        t1 = tok.astype(BF16)
        r1 = tok - t1.astype(F32)
        t2 = r1.astype(BF16)
        tok3_ref[0] = t1
        tok3_ref[1] = t2
        tok3_ref[2] = (r1 - t2.astype(F32)).astype(BF16)
        acc_ref[...] = jnp.zeros(acc_ref.shape, F32)

    gf = g.astype(F32)
    in_g_row = rowv_ref[0:1, :] == gf
    pos_row = rowv_ref[1:2, :]
    tok = tok_ref[...]
    in_g_col = tok[:, GRP_LANE:GRP_LANE + 1] == gf
    pos_col = tok[:, POS_LANE:POS_LANE + 1]
    r_sub = lax.broadcasted_iota(jnp.int32, (MOE_CHUNK, tm), 0).astype(F32)
    r_lane = lax.broadcasted_iota(jnp.int32, (tm, MOE_CHUNK), 1).astype(F32)
    lane = lax.broadcasted_iota(jnp.int32, (MOE_CHUNK, LANES), 1)

    def chunk(c, carry):
        base = (c * MOE_CHUNK).astype(F32)
        gather = (in_g_row & (pos_row - base == r_sub)).astype(BF16)
        scatter = (in_g_col & (pos_col - base == r_lane)).astype(BF16)
        xg = _dot(gather, xb_ref[...]).astype(BF16)
        cg = _dot(gather, tok3_ref[0]) + _dot(gather, tok3_ref[1]) + _dot(gather, tok3_ref[2])
        y = jnp.zeros((MOE_CHUNK, x_ref.shape[1]), F32)
        for k in range(EXPERTS_PER_GROUP):
            col = jnp.sum(jnp.where(lane == g * EXPERTS_PER_GROUP + k, cg, 0.0), -1, keepdims=True)
            hid = jax.nn.silu(_dot(xg, w1_ref[k])) * _dot(xg, w3_ref[k]) * col
            y = y + _dot(hid.astype(BF16), w2_ref[k])
        acc_ref[...] += _dot(scatter, y.astype(BF16))
        return carry

    lax.fori_loop(0, (cnt_ref[g] + MOE_CHUNK - 1) // MOE_CHUNK, chunk, 0)

    @pl.when(g == N_GROUPS - 1)
    def _finish():
        hn = _layer_norm(alpha * x_ref[...] + acc_ref[...], g_ref[...], b_ref[...])
        of_ref[...] = hn
        ob_ref[...] = hn.astype(BF16)


def _moe_call(h, hb, rwt, rb, w1, w3, w2, layer, g, b, alpha, tm):
    T, D = h.shape
    F = w1.shape[2]
    first_group = layer * N_GROUPS
    row = pl.BlockSpec((tm, D), lambda i, e: (i, 0))
    idx = jnp.arange(tm)
    tri = (idx[:, None] < idx[None, :]).astype(BF16)

    def full(a):
        return pl.BlockSpec(a.shape, lambda i, e: (0, 0))

    def weights(rows, cols):
        return pl.BlockSpec((EXPERTS_PER_GROUP, rows, cols), lambda i, e: (first_group + e, 0, 0))

    return pl.pallas_call(
        functools.partial(_moe_kernel, alpha=alpha, tm=tm), grid=(T // tm, N_GROUPS),
        in_specs=[row, row, full(rwt), full(rb), full(tri), weights(D, F), weights(D, F), weights(F, D),
                  full(g), full(b)],
        out_specs=[row, row],
        out_shape=[jax.ShapeDtypeStruct((T, D), F32), jax.ShapeDtypeStruct((T, D), BF16)],
        scratch_shapes=[pltpu.VMEM((tm, LANES), F32), pltpu.VMEM((3, tm, LANES), BF16), pltpu.VMEM((8, tm), F32),
                        pltpu.VMEM((tm, D), F32), pltpu.SMEM((N_GROUPS,), jnp.int32)],
        compiler_params=_params(("parallel", "arbitrary")), name="moe")(h, hb, rwt, rb, tri, w1, w3, w2, g, b)


def _head_pad_cols(w, n_heads, width, scale=1.0):
    K = w.shape[0]
    w = (w * scale).reshape(K, n_heads, width)
    return jnp.pad(w, ((0, 0), (0, 0), (0, LANES - width))).reshape(K, n_heads * LANES)


def _head_pad_rows(w, n_heads, width):
    N = w.shape[1]
    w = w.reshape(n_heads, width, N)
    return jnp.pad(w, ((0, 0), (0, LANES - width), (0, 0))).reshape(n_heads * LANES, N)


def _rot_half_cols(w):
    half = w.shape[1] // 2
    return jnp.concatenate([-w[:, half:], w[:, :half]], axis=1)


def _in_proj_weights(w_in):
    D = w_in.shape[0]
    widths = (NSA_HEADS * HEAD_DIM,) + (NSA_GROUPS * HEAD_DIM,) * 6 + (
        3 * NSA_HEADS, DIFF_HEADS * 2 * DIFF_DIM, DIFF_HEADS * 2 * DIFF_DIM, DIFF_HEADS * 2 * DIFF_DIM,
        Q_LORA, KV_LORA, MLA_ROPE, 3 * D)
    parts, o = [], 0
    for w in widths:
        parts.append(w_in[:, o:o + w])
        o += w
    nq, kc, vc, ks, vs, kw, vw, ng, dq, dk, dv, cq, ckv, kr, mg = parts
    wb = jnp.concatenate([
        _head_pad_cols(nq, NSA_HEADS, HEAD_DIM, LOG2E * HEAD_DIM ** -0.5),
        _head_pad_cols(dq, 2 * DIFF_HEADS, DIFF_DIM, LOG2E * DIFF_DIM ** -0.5),
        _head_pad_cols(dk, 2 * DIFF_HEADS, DIFF_DIM),
        _head_pad_cols(ks, NSA_GROUPS, HEAD_DIM), _head_pad_cols(vs, NSA_GROUPS, HEAD_DIM),
        _head_pad_cols(kw, NSA_GROUPS, HEAD_DIM), _head_pad_cols(vw, NSA_GROUPS, HEAD_DIM),
        dv], axis=1).astype(BF16)

    def rope_block(w):
        return jnp.pad(w, ((0, 0), (MLA_NOPE, LANES - MLA_NOPE - MLA_ROPE)))

    wf = jnp.concatenate([
        cq, kc, vc, jnp.pad(ng, ((0, 0), (0, LANES - ng.shape[1]))), ckv,
        rope_block(kr), rope_block(_rot_half_cols(kr)), mg], axis=1).astype(BF16)
    assert wb.shape[1] == ZB_WIDTH and wf.shape[1] == ZF_WIDTH
    return wb, wf


def _bf16_terms(x, n):
    terms, rest = [], np.float32(x)
    for _ in range(n):
        t = np.float32(np.asarray(rest, dtype=BF16).astype(np.float32))
        terms.append(float(t))
        rest = np.float32(rest - t)
    return terms


def _aux_const_row(slopes):
    row = np.zeros((1, ZB_WIDTH), np.float32)
    q_blocks = [(ZB_QN + h * LANES, slopes[h]) for h in range(NSA_HEADS)]
    q_blocks += [(ZB_DQ + b * LANES, slopes[NSA_HEADS + b // 2]) for b in range(2 * DIFF_HEADS)]
    for off, slope in q_blocks:
        for n, term in enumerate(_bf16_terms(slope, ALIBI_TERMS)):
            row[0, off + AUX_LANE + 2 * n] = ALIBI_RADIX * term
            row[0, off + AUX_LANE + 2 * n + 1] = term
    for off in (ZB_VS, ZB_VW):
        for g in range(NSA_GROUPS):
            row[0, off + g * LANES + AUX_LANE] = 1.0
    return jnp.asarray(row)


def _key_offset_table(tk):
    tab = np.zeros((tk, LANES), np.float32)
    c = np.arange(tk)
    for n in range(ALIBI_TERMS):
        tab[:, AUX_LANE + 2 * n] = c // ALIBI_RADIX
        tab[:, AUX_LANE + 2 * n + 1] = c % ALIBI_RADIX
    return jnp.asarray(tab, BF16)


def _rope_tables(S):
    half = MLA_ROPE // 2
    freqs = ROPE_THETA ** (-jnp.arange(half, dtype=F32) / half)
    ang = jnp.arange(S, dtype=F32)[:, None] * freqs[None, :]
    cos = jnp.concatenate([jnp.cos(ang), jnp.cos(ang)], -1)
    sin = jnp.concatenate([jnp.sin(ang), jnp.sin(ang)], -1)
    tail = jnp.zeros((S, LANES - MLA_NOPE - MLA_ROPE), F32)
    cos_q = jnp.concatenate([jnp.ones((S, MLA_NOPE), F32), cos, tail], -1)
    cos_k = jnp.concatenate([jnp.zeros((S, MLA_NOPE), F32), cos, tail], -1)
    sin_qk = jnp.concatenate([jnp.zeros((S, MLA_NOPE), F32), sin, tail], -1)
    return cos_q, sin_qk, cos_k, sin_qk


def _score_matrix_t(S):
    nch = S // CMP_STRIDE
    ratio = CMP_LEN // CMP_STRIDE
    per_sb = SEL_BLOCK // CMP_STRIDE
    sb = jnp.arange(LANES)[:, None]
    cb = jnp.arange(nch)[None, :]
    m = jnp.zeros((LANES, nch), F32)
    for jj in range(ratio):
        chunk = cb + jj
        m = m + ((chunk // per_sb == sb) & (chunk < nch)).astype(F32)
    return m.astype(BF16)


def _cmp_flat(z, B, S):
    nch = S // CMP_STRIDE
    x = z.reshape(B, S, NSA_GROUPS, HEAD_DIM).transpose(0, 2, 1, 3).reshape(B, NSA_GROUPS, nch, CMP_STRIDE * HEAD_DIM)
    nxt = jnp.roll(x, -1, axis=2)
    return jnp.concatenate([x, nxt], -1).reshape(B * NSA_GROUPS * nch, CMP_LEN * HEAD_DIM)


def _mixer_and_ffn(h, hb, layer, B, S, p, shared, alpha):
    T, D = h.shape
    tq = min(WINDOW, S)
    wide_tk = 2 * tq if S % (2 * tq) == 0 else tq
    wb, wf = _in_proj_weights(p["w_in"])
    zb = _matmul(hb, wb, BF16, min(1024, T), ZB_WIDTH // 3, "in_proj_b", const_row=shared["aux_row"])
    zf = _matmul(hb, wf, F32, min(1024, T), ZF_WIDTH // 4, "in_proj_f")
    slopes = _alibi_slopes_log2()
    nsa_slopes, diff_slopes = slopes[:NSA_HEADS], slopes[NSA_HEADS:]

    nch = S // CMP_STRIDE
    cmp_out = []
    for off, pos, w1, w2 in ((ZF_KC, p["cmp_pos_k"], p["cmp_w1_k"], p["cmp_w2_k"]),
                             (ZF_VC, p["cmp_pos_v"], p["cmp_w1_v"], p["cmp_w2_v"])):
        flat = _cmp_flat(zf[:, off:off + NSA_GROUPS * HEAD_DIM], B, S)
        w2p = jnp.pad(w2, ((0, 0), (0, LANES - HEAD_DIM))).astype(BF16)
        out = _compress_call(flat, pos.reshape(1, CMP_LEN * HEAD_DIM), w1.astype(BF16), w2p,
                             min(512, flat.shape[0]), "nsa_compress")
        cmp_out.append(out.reshape(B, NSA_GROUPS, nch, LANES))
    o_cmp, sel, any_sel = _cmp_call(B, S, zb, cmp_out[0], cmp_out[1], shared["score_t"], min(256, S))
    grp_map = [h_ // NSA_HPG for h_ in range(NSA_HEADS)]
    sel_steps = _selected_steps(any_sel, S, tq)

    def selected_sweep(steps):
        return _flash_call(B, S, zb, ZB_QN, NSA_HEADS, zb, ZB_KS, NSA_GROUPS, zb, ZB_VS, NSA_GROUPS, shared,
                           kmap=grp_map, vmap=grp_map, slopes=nsa_slopes, mode="causal", tq=tq, name="nsa_sel",
                           sel=sel, steps=steps)

    n_short = sel_steps[0].shape[1] // 2
    n_visited = jnp.max(jnp.sum(sel_steps[2] & STEP_ACTIVE, axis=1))
    o_sel = lax.cond(n_visited <= n_short,
                     lambda: selected_sweep(tuple(a[:, :n_short] for a in sel_steps)),
                     lambda: selected_sweep(sel_steps))
    o_win = _flash_call(B, S, zb, ZB_QN, NSA_HEADS, zb, ZB_KW, NSA_GROUPS, zb, ZB_VW, NSA_GROUPS, shared,
                        kmap=grp_map, vmap=grp_map, slopes=nsa_slopes, mode="window", tq=tq, name="nsa_win")

    lam_init = 0.8 - 0.6 * math.exp(-0.3 * layer)
    n_maps = 2 * DIFF_HEADS
    o_diff = _flash_call(B, S, zb, ZB_DQ, n_maps, zb, ZB_DK, n_maps, zb, ZB_DV, DIFF_HEADS, shared,
                         kmap=list(range(n_maps)), vmap=[m_ // 2 for m_ in range(n_maps)],
                         slopes=[diff_slopes[m_ // 2] for m_ in range(n_maps)],
                         mode="causal", tq=tq, tk=wide_tk, name="diff_attn", lam=p["diff_lambda"],
                         subg=p["diff_subln_g"].reshape(1, 2 * DIFF_DIM), lam_init=lam_init)

    w_uq = p["mla_w_uq"].reshape(Q_LORA, MLA_HEADS, MLA_NOPE + MLA_ROPE)
    wq = jnp.pad(w_uq, ((0, 0), (0, 0), (0, LANES - MLA_NOPE - MLA_ROPE))).reshape(Q_LORA, MLA_HEADS * LANES)
    rot = jnp.stack([_rot_half_cols(w_uq[:, h_, MLA_NOPE:]) for h_ in range(MLA_HEADS)], axis=1)
    wqr = jnp.pad(rot, ((0, 0), (0, 0), (MLA_NOPE, LANES - MLA_NOPE - MLA_ROPE))).reshape(Q_LORA, MLA_HEADS * LANES)
    w_ukv = p["mla_w_ukv"].reshape(KV_LORA, MLA_HEADS, MLA_NOPE + MLA_V)
    wk = _head_pad_cols(w_ukv[:, :, :MLA_NOPE].reshape(KV_LORA, -1), MLA_HEADS, MLA_NOPE)
    wv = _head_pad_cols(w_ukv[:, :, MLA_NOPE:].reshape(KV_LORA, -1), MLA_HEADS, MLA_V)
    qm, km, vm = _mla_prep_call(zf, shared["rope"], p["mla_q_norm_g"].reshape(1, Q_LORA),
                                p["mla_kv_norm_g"].reshape(1, KV_LORA), wq.astype(BF16), wqr.astype(BF16),
                                wk.astype(BF16), wv.astype(BF16), S, min(512, S))
    ident = list(range(MLA_HEADS))
    o_mla = _flash_call(B, S, qm, 0, MLA_HEADS, km, 0, MLA_HEADS, vm, 0, MLA_HEADS, shared, kmap=ident, vmap=ident,
                        slopes=[0.0] * MLA_HEADS, mode="causal", tq=tq, tk=wide_tk, name="mla_attn")

    h1, h1b = _combine_call(
        o_cmp, o_sel, o_win, zf, o_diff, o_mla, h, shared["gate_expand"],
        _head_pad_rows(p["w_br_nsa"], NSA_HEADS, HEAD_DIM).astype(BF16), p["w_br_diff"].astype(BF16),
        _head_pad_rows(p["w_br_mla"], MLA_HEADS, MLA_V).astype(BF16), p["w_out"].astype(BF16),
        p["ln1_g"].reshape(1, D), p["ln1_b"].reshape(1, D), alpha, min(512, T))

    return _moe_call(h1, h1b, shared["router_wt"], shared["router_b"], shared["moe_w1"],
                     shared["moe_w3"], shared["moe_w2"], layer, p["ln2_g"].reshape(1, D),
                     p["ln2_b"].reshape(1, D), alpha, min(512, T))


def kernel(x, ln_in_g, ln_in_b, w_in, cmp_pos_k, cmp_w1_k, cmp_w2_k, cmp_pos_v, cmp_w1_v, cmp_w2_v, diff_lambda, diff_subln_g, mla_q_norm_g, mla_kv_norm_g, mla_w_uq, mla_w_ukv, w_br_nsa, w_br_diff, w_br_mla, w_out, ln1_g, ln1_b, router_w, router_b, moe_w1, moe_w3, moe_w2, ln2_g, ln2_b):
    B, S, D = x.shape
    depth = w_in.shape[0]
    alpha = (2 * depth) ** 0.25
    T = B * S
    tq = min(WINDOW, S)
    idx = jnp.arange(tq, dtype=jnp.int32)
    gate_rows = jnp.arange(LANES)[:, None]
    gate_cols = jnp.arange(NSA_HEADS * LANES)[None, :] // LANES
    shared = {
        "rel": (idx[None, :] - idx[:, None]).astype(F32),
        "aux_row": _aux_const_row(_alibi_slopes_log2()),
        "block_onehot": ((jnp.arange(S)[:, None] // SEL_BLOCK) == jnp.arange(LANES)[None, :]).astype(BF16),
        "score_t": _score_matrix_t(S),
        "rope": _rope_tables(S),
        "gate_expand": jnp.stack([(gate_rows == gate_cols * 3 + j) for j in range(3)]).astype(BF16),
        "router_wt": router_w.T.astype(BF16),
        "router_b": router_b.reshape(N_EXPERTS, 1).astype(F32),
        "moe_w1": moe_w1.astype(BF16).reshape((depth * N_EXPERTS,) + moe_w1.shape[2:]),
        "moe_w3": moe_w3.astype(BF16).reshape((depth * N_EXPERTS,) + moe_w3.shape[2:]),
        "moe_w2": moe_w2.astype(BF16).reshape((depth * N_EXPERTS,) + moe_w2.shape[2:]),
    }
    per_layer = dict(w_in=w_in, cmp_pos_k=cmp_pos_k, cmp_w1_k=cmp_w1_k, cmp_w2_k=cmp_w2_k, cmp_pos_v=cmp_pos_v,
                     cmp_w1_v=cmp_w1_v, cmp_w2_v=cmp_w2_v, diff_lambda=diff_lambda, diff_subln_g=diff_subln_g,
                     mla_q_norm_g=mla_q_norm_g, mla_kv_norm_g=mla_kv_norm_g, mla_w_uq=mla_w_uq, mla_w_ukv=mla_w_ukv,
                     w_br_nsa=w_br_nsa, w_br_diff=w_br_diff, w_br_mla=w_br_mla, w_out=w_out, ln1_g=ln1_g,
                     ln1_b=ln1_b, ln2_g=ln2_g, ln2_b=ln2_b)
    h, hb = _ln_call(x.reshape(T, D), ln_in_g, ln_in_b, min(512, T))
    for l in range(depth):
        p = {k: v[l] for k, v in per_layer.items()}
        h, hb = _mixer_and_ffn(h, hb, l, B, S, p, shared, alpha)
    return h.reshape(B, S, D)
```
